```python
import math
import jax, jax.numpy as jnp
from jax import lax
import numpy as np

D_MODEL = 1024
BATCH = 16
SEQ = 2048
DEPTH = 2

CHUNK = 64
N_META = 16
BLOCK = math.gcd(N_META, CHUNK)
N_MIXERS = 2
N_LAYERS_A = (DEPTH + N_MIXERS - 1) // N_MIXERS
N_LAYERS_B = DEPTH // N_MIXERS
EPS = 1e-6

D_FF = 2816

A_HEADS = 8
A_DK = D_MODEL // 2 // A_HEADS
A_DV = D_MODEL // A_HEADS
A_CONV = 4
A_QK = A_HEADS * A_DK
A_V = A_HEADS * A_DV
A_COLS = 2 * A_QK + 2 * A_V + 2 * A_HEADS

B_HEADS = 8
B_DK = 128
B_DV = D_MODEL // B_HEADS
B_F = B_HEADS * B_DK
B_I = B_HEADS * B_DV
B_COLS = 2 * B_F + 2 * B_I

kernel_name = 'hybrid_mlstm_hgrn2_macaron_trunk'


def rmsnorm(x, g):
    xf = x.astype(jnp.float32)
    y = xf * lax.rsqrt(jnp.mean(xf * xf, axis=-1, keepdims=True) + EPS)
    return (y * g.astype(jnp.float32)).astype(x.dtype)


def swiglu(x, w_in, w_out):
    gu = x @ w_in
    return (jax.nn.silu(gu[..., :D_FF]) * gu[..., D_FF:]) @ w_out


def causal_conv(x, w, b):
    kw, c = w.shape
    y = lax.conv_general_dilated(x, w[:, None, :].astype(x.dtype), window_strides=(1,),
                                 padding=[(kw - 1, 0)], dimension_numbers=('NWC', 'WIO', 'NWC'),
                                 feature_group_count=c)
    return y + b.astype(y.dtype)


def to_blocks(x):
    bsz, t = x.shape[0], x.shape[1]
    x = x.reshape((bsz, t // BLOCK, BLOCK) + x.shape[2:])
    return jnp.swapaxes(jnp.moveaxis(x, 1, 0), 2, 3)


def from_blocks(x):
    nb, bsz, h, l, d = x.shape
    x = jnp.moveaxis(jnp.swapaxes(x, 2, 3), 0, 1)
    return x.reshape(bsz, nb * l, h * d)


def mlstm_scan(q, k, v, ig, lf):
    _, bsz, nh, l, dk = q.shape
    dv = v.shape[-1]
    causal = jnp.tril(jnp.ones((l, l), dtype=bool))

    def step(carry, xs):
        c_prev, n_prev, m_prev = carry
        qb, kb, vb, ib, fb = xs
        bcum = jnp.cumsum(fb, axis=-1)
        dmat = jnp.where(causal, bcum[..., :, None] - bcum[..., None, :] + ib[..., None, :], -jnp.inf)
        g_inter = bcum + m_prev[..., None]
        m = jnp.maximum(g_inter, jnp.max(dmat, axis=-1))
        w_intra = jnp.exp(dmat - m[..., None])
        w_inter = jnp.exp(g_inter - m)
        s = jnp.einsum('bhld,bhsd->bhls', qb, kb) * w_intra
        num = jnp.einsum('bhls,bhsv->bhlv', s, vb) + w_inter[..., None] * jnp.einsum('bhvd,bhld->bhlv', c_prev, qb)
        den = jnp.sum(s, axis=-1) + w_inter * jnp.einsum('bhd,bhld->bhl', n_prev, qb)
        hb = num / jnp.maximum(jnp.abs(den), jnp.exp(-m))[..., None]
        b_last = bcum[..., -1]
        g_state = b_last[..., None] - bcum + ib
        m_new = jnp.maximum(b_last + m_prev, jnp.max(g_state, axis=-1))
        w_s = jnp.exp(g_state - m_new[..., None])
        decay = jnp.exp(b_last + m_prev - m_new)
        c_new = decay[..., None, None] * c_prev + jnp.einsum('bhsv,bhsd->bhvd', vb * w_s[..., None], kb)
        n_new = decay[..., None] * n_prev + jnp.einsum('bhs,bhsd->bhd', w_s, kb)
        return (c_new, n_new, m_new), hb

    init = (jnp.zeros((bsz, nh, dv, dk), jnp.float32), jnp.zeros((bsz, nh, dk), jnp.float32),
            jnp.zeros((bsz, nh), jnp.float32))
    _, hs = lax.scan(step, init, (q, k, v, ig, lf))
    return hs


def hgrn2_scan(q, k, v, lf):
    _, bsz, nh, l, dk = q.shape
    dv = v.shape[-1]
    causal = jnp.tril(jnp.ones((l, l), dtype=bool))

    def step(s_prev, xs):
        qb, kb, vb, fb = xs
        bcum = jnp.cumsum(fb, axis=-2)
        rel = jnp.where(causal[:, :, None], bcum[..., :, None, :] - bcum[..., None, :, :], -jnp.inf)
        a = jnp.sum(qb[..., :, None, :] * jnp.exp(rel) * kb[..., None, :, :], axis=-1)
        o = jnp.einsum('bhls,bhsv->bhlv', a, vb) + jnp.einsum('bhld,bhdv->bhlv', qb * jnp.exp(bcum), s_prev)
        b_last = bcum[..., -1, :]
        s_new = jnp.exp(b_last)[..., None] * s_prev + jnp.einsum(
            'bhsd,bhsv->bhdv', kb * jnp.exp(b_last[..., None, :] - bcum), vb)
        return s_new, o

    _, os_ = lax.scan(step, jnp.zeros((bsz, nh, dk, dv), jnp.float32), (q, k, v, lf))
    return os_


def mlstm_mixer(u, w_in, conv_w, conv_b, gate_b, w_out):
    bsz, t, _ = u.shape
    z = u @ w_in
    qk = jax.nn.silu(causal_conv(z[..., :2 * A_QK], conv_w, conv_b))
    v = z[..., 2 * A_QK:2 * A_QK + A_V]
    og = z[..., 2 * A_QK + A_V:2 * A_QK + 2 * A_V]
    gates = z[..., 2 * A_QK + 2 * A_V:].astype(jnp.float32) + gate_b.astype(jnp.float32)
    q = qk[..., :A_QK].reshape(bsz, t, A_HEADS, A_DK).astype(jnp.float32) * (A_DK ** -0.5)
    k = qk[..., A_QK:].reshape(bsz, t, A_HEADS, A_DK).astype(jnp.float32)
    v = v.reshape(bsz, t, A_HEADS, A_DV).astype(jnp.float32)
    ig = gates[..., :A_HEADS]
    lf = jax.nn.log_sigmoid(gates[..., A_HEADS:])
    hs = mlstm_scan(to_blocks(q), to_blocks(k), to_blocks(v), to_blocks(ig), to_blocks(lf))
    y = from_blocks(hs).astype(u.dtype) * jax.nn.sigmoid(og)
    return y @ w_out


def hgrn2_mixer(u, w_in, f_bias, lower_bound, g_norm, w_out):
    bsz, t, _ = u.shape
    z = u @ w_in
    q = jax.nn.silu(z[..., :B_F])
    fpre = z[..., B_F:2 * B_F].astype(jnp.float32) + f_bias.astype(jnp.float32)
    i_in = z[..., 2 * B_F:2 * B_F + B_I]
    g = z[..., 2 * B_F + B_I:]
    lf = jnp.logaddexp(jnp.log(lower_bound), jnp.log1p(-lower_bound) + jax.nn.log_sigmoid(fpre))
    k = (1.0 - lower_bound) * jax.nn.sigmoid(-fpre)
    q = q.reshape(bsz, t, B_HEADS, B_DK).astype(jnp.float32)
    k = k.reshape(bsz, t, B_HEADS, B_DK)
    lf = lf.reshape(bsz, t, B_HEADS, B_DK)
    v = i_in.reshape(bsz, t, B_HEADS, B_DV).astype(jnp.float32)
    o = hgrn2_scan(to_blocks(q), to_blocks(k), to_blocks(v), to_blocks(lf))
    y = from_blocks(o).astype(u.dtype) * jax.nn.sigmoid(g)
    return rmsnorm(y, g_norm) @ w_out


def setup_inputs(seed: int = 0) -> dict:
    key = jax.random.key(seed)
    ks = jax.random.split(key, 16)
    f32 = jnp.float32
    x = jax.random.normal(ks[0], (BATCH, SEQ, D_MODEL), f32)
    meta_tokens = jax.random.normal(ks[1], (N_META, D_MODEL), f32)
    norm_gains = 1.0 + 0.1 * jax.random.normal(ks[2], (DEPTH, 6, D_MODEL), f32)
    ffn_w_in = jax.random.normal(ks[3], (DEPTH, 2, D_MODEL, 2 * D_FF), f32) * D_MODEL ** -0.5
    ffn_w_out = jax.random.normal(ks[4], (DEPTH, 2, D_FF, D_MODEL), f32) * D_FF ** -0.5
    a_w_in = jax.random.normal(ks[5], (N_LAYERS_A, D_MODEL, A_COLS), f32) * D_MODEL ** -0.5
    a_conv_w = jax.random.normal(ks[6], (N_LAYERS_A, A_CONV, 2 * A_QK), f32) * A_CONV ** -0.5
    a_conv_b = 0.01 * jax.random.normal(ks[7], (N_LAYERS_A, 2 * A_QK), f32)
    i_bias = 0.1 * jax.random.normal(ks[8], (N_LAYERS_A, A_HEADS), f32)
    f_bias_a = jnp.linspace(3.0, 6.0, A_HEADS, dtype=f32)[None, :] + 0.1 * jax.random.normal(ks[9], (N_LAYERS_A, A_HEADS), f32)
    a_gate_b = jnp.concatenate([i_bias, f_bias_a], axis=-1)
    a_w_out = jax.random.normal(ks[10], (N_LAYERS_A, A_V, D_MODEL), f32) * A_V ** -0.5
    b_w_in = jax.random.normal(ks[11], (N_LAYERS_B, D_MODEL, B_COLS), f32) * D_MODEL ** -0.5
    b_f_bias = 0.1 * jax.random.normal(ks[12], (N_LAYERS_B, B_F), f32)
    b_lb_raw = 0.5 * jax.random.normal(ks[13], (DEPTH, B_F), f32)
    b_g_norm = 1.0 + 0.1 * jax.random.normal(ks[14], (N_LAYERS_B, B_I), f32)
    b_w_out = jax.random.normal(ks[15], (N_LAYERS_B, B_I, D_MODEL), f32) * B_I ** -0.5
    return {'x': x, 'meta_tokens': meta_tokens, 'norm_gains': norm_gains, 'ffn_w_in': ffn_w_in,
            'ffn_w_out': ffn_w_out, 'a_w_in': a_w_in, 'a_conv_w': a_conv_w, 'a_conv_b': a_conv_b,
            'a_gate_b': a_gate_b, 'a_w_out': a_w_out, 'b_w_in': b_w_in, 'b_f_bias': b_f_bias,
            'b_lb_raw': b_lb_raw, 'b_g_norm': b_g_norm, 'b_w_out': b_w_out}


def reference(x, meta_tokens, norm_gains, ffn_w_in, ffn_w_out, a_w_in, a_conv_w, a_conv_b, a_gate_b,
              a_w_out, b_w_in, b_f_bias, b_lb_raw, b_g_norm, b_w_out):
    bsz = x.shape[0]
    meta = jnp.broadcast_to(meta_tokens[None].astype(x.dtype), (bsz, N_META, D_MODEL))
    h = jnp.concatenate([meta, x], axis=1)
    sm = jax.nn.softmax(b_lb_raw.astype(jnp.float32), axis=0)
    lb_all = jnp.cumsum(sm, axis=0) - sm[0]
    for i in range(DEPTH):
        gn = norm_gains[i]
        h = h + 0.5 * rmsnorm(swiglu(rmsnorm(h, gn[0]), ffn_w_in[i, 0], ffn_w_out[i, 0]), gn[1])
        u = rmsnorm(h, gn[2])
        j = i // N_MIXERS
        if i % N_MIXERS == 0:
            y = mlstm_mixer(u, a_w_in[j], a_conv_w[j], a_conv_b[j], a_gate_b[j], a_w_out[j])
        else:
            y = hgrn2_mixer(u, b_w_in[j], b_f_bias[j], lb_all[i], b_g_norm[j], b_w_out[j])
        h = h + rmsnorm(y, gn[3])
        h = h + 0.5 * rmsnorm(swiglu(rmsnorm(h, gn[4]), ffn_w_in[i, 1], ffn_w_out[i, 1]), gn[5])
    return h[:, N_META:]
```

```python
import functools
import math

import jax
import jax.numpy as jnp
from jax import lax
from jax.experimental import pallas as pl
from jax.experimental.pallas import tpu as pltpu

LANES = 128
SUBLANES = 8
VMEM_LIMIT_BYTES = 56 * 1024 * 1024

N_META = 16
EPS = 1e-6
A_HEADS = 8
A_DK = 64
A_CONV = 4
B_HEADS = 8
HEAD_W = 128
SCAN_CHUNK = 128
FF_TILE = 256
MASKED = -1e30

F32 = jnp.float32
BF16 = jnp.bfloat16
NT = (((1,), (1,)), ((), ()))
TN = (((0,), (0,)), ((), ()))


def _rmsnorm(x, g):
    return x * lax.rsqrt(jnp.mean(x * x, axis=-1, keepdims=True) + EPS) * g


def _sigmoid(x):
    return 1.0 / (1.0 + jnp.exp(-x))


def _log_sigmoid(x):
    return jnp.minimum(x, 0.0) - jnp.log1p(jnp.exp(-jnp.abs(x)))


def _params(n_grid_axes):
    return pltpu.CompilerParams(dimension_semantics=("arbitrary",) * n_grid_axes,
                                vmem_limit_bytes=VMEM_LIMIT_BYTES)


def _const_spec(shape):
    zeros = (0,) * len(shape)
    return pl.BlockSpec(shape, lambda *_: zeros, pipeline_mode=pl.Buffered(1))


def _ffn_kernel(h_ref, gpre_ref, gpost_ref, win_ref, wout_ref, o_ref, acc_ref, u_ref, *, n_chunks, tf):
    x = h_ref[...]
    u_ref[...] = _rmsnorm(x, gpre_ref[...]).astype(BF16)
    acc_ref[...] = jnp.zeros_like(acc_ref)

    def body(c, carry):
        gu = jnp.dot(u_ref[...], win_ref[c], preferred_element_type=F32)
        gate = gu[:, :tf]
        act = gate * _sigmoid(gate) * gu[:, tf:]
        acc_ref[...] += jnp.dot(act.astype(BF16), wout_ref[c], preferred_element_type=F32)
        return carry

    lax.fori_loop(0, n_chunks, body, 0)
    o_ref[...] = x + 0.5 * _rmsnorm(acc_ref[...], gpost_ref[...])


def _ffn(h, g_pre, g_post, w_in_r, w_out_r, tm):
    m, d = h.shape
    n_chunks, _, tf2 = w_in_r.shape
    row = pl.BlockSpec((tm, d), lambda i: (i, 0))
    return pl.pallas_call(
        functools.partial(_ffn_kernel, n_chunks=n_chunks, tf=tf2 // 2),
        grid=(m // tm,),
        in_specs=[row, _const_spec((1, d)), _const_spec((1, d)),
                  _const_spec(w_in_r.shape), _const_spec(w_out_r.shape)],
        out_specs=row,
        out_shape=jax.ShapeDtypeStruct((m, d), F32),
        scratch_shapes=[pltpu.VMEM((tm, d), F32), pltpu.VMEM((tm, d), BF16)],
        compiler_params=_params(1),
        name="ffn",
    )(h, g_pre, g_post, w_in_r, w_out_r)


def _shift_rows(x, j, prev):
    r = pltpu.roll(x, j, 0)
    rp = pltpu.roll(prev, j, 0)
    row = lax.broadcasted_iota(jnp.int32, (SUBLANES, x.shape[1]), 0)
    head = jnp.where(row < j, rp, r[:SUBLANES])
    if x.shape[0] == SUBLANES:
        return head
    return jnp.concatenate([head, r[SUBLANES:]], axis=0)


def _a_proj_kernel(h_ref, g_ref, w_ref, wg_ref, cw_ref, cb_ref, gb_ref, tail0_ref,
                   q_ref, k_ref, v_ref, og_ref, gates_ref, tail_ref, tail_scr, *, tiles_per_seq):
    d_qk = A_HEADS * A_DK
    d_v = A_HEADS * HEAD_W

    @pl.when(pl.program_id(0) % tiles_per_seq == 0)
    def _():
        tail_scr[...] = tail0_ref[...]

    u = _rmsnorm(h_ref[...], g_ref[...]).astype(BF16)
    zqk = jnp.dot(u, w_ref[:, :2 * d_qk], preferred_element_type=F32)
    prev = tail_scr[...]
    cw = cw_ref[...]
    acc = cb_ref[...] + cw[A_CONV - 1:A_CONV, :] * zqk
    for j in range(1, A_CONV):
        acc = acc + cw[A_CONV - 1 - j:A_CONV - j, :] * _shift_rows(zqk, j, prev)
    tm = zqk.shape[0]
    tail_scr[...] = zqk[tm - SUBLANES:, :]
    tail_ref[...] = zqk[tm - SUBLANES:, :]
    qk = acc * _sigmoid(acc)
    q_ref[...] = (qk[:, :d_qk] * (A_DK ** -0.5)).astype(BF16)
    k_ref[...] = qk[:, d_qk:].astype(BF16)
    v_ref[...] = jnp.dot(u, w_ref[:, 2 * d_qk:2 * d_qk + d_v], preferred_element_type=F32).astype(BF16)
    og = jnp.dot(u, w_ref[:, 2 * d_qk + d_v:], preferred_element_type=F32)
    og_ref[...] = _sigmoid(og).astype(BF16)
    g = jnp.dot(u, wg_ref[...], preferred_element_type=F32) + gb_ref[...]
    lane = lax.broadcasted_iota(jnp.int32, g.shape, 1)
    gates_ref[...] = jnp.where(lane < A_HEADS, g, _log_sigmoid(g))


def _a_proj(h, g, w, wg, cw, cb, gb, tail0, tm, seq):
    m, d = h.shape
    d_qk = A_HEADS * A_DK
    d_v = A_HEADS * HEAD_W
    row = lambda c: pl.BlockSpec((tm, c), lambda i: (i, 0))
    return pl.pallas_call(
        functools.partial(_a_proj_kernel, tiles_per_seq=seq // tm),
        grid=(m // tm,),
        in_specs=[row(d), _const_spec((1, d)), _const_spec(w.shape), _const_spec(wg.shape),
                  _const_spec(cw.shape), _const_spec(cb.shape), _const_spec(gb.shape), _const_spec(tail0.shape)],
        out_specs=[row(d_qk), row(d_qk), row(d_v), row(d_v), row(LANES),
                   pl.BlockSpec((SUBLANES, 2 * d_qk), lambda i: (0, 0))],
        out_shape=[jax.ShapeDtypeStruct((m, d_qk), BF16), jax.ShapeDtypeStruct((m, d_qk), BF16),
                   jax.ShapeDtypeStruct((m, d_v), BF16), jax.ShapeDtypeStruct((m, d_v), BF16),
                   jax.ShapeDtypeStruct((m, LANES), F32), jax.ShapeDtypeStruct((SUBLANES, 2 * d_qk), F32)],
        scratch_shapes=[pltpu.VMEM((SUBLANES, 2 * d_qk), F32)],
        compiler_params=_params(1),
        name="mlstm_proj",
    )(h, g, w, wg, cw, cb, gb, tail0)


def _a_scan_kernel(q_ref, k_ref, v_ref, og_ref, gates_ref, c0_ref, n0_ref, m0_ref,
                   y_ref, cfin_ref, nfin_ref, mfin_ref, c_scr, n_scr, m_scr):
    L = SCAN_CHUNK
    chunk = pl.program_id(1)

    @pl.when(chunk == 0)
    def _():
        c_scr[...] = c0_ref[...]
        n_scr[...] = n0_ref[...]
        m_scr[...] = m0_ref[...]

    gates = gates_ref[...]
    lane = lax.broadcasted_iota(jnp.int32, (L, LANES), 1)
    r = lax.broadcasted_iota(jnp.int32, (L, L), 0)
    s = lax.broadcasted_iota(jnp.int32, (L, L), 1)
    causal = s <= r
    tri = jnp.where(causal, 1.0, 0.0).astype(F32)
    is_f = (lane >= A_HEADS) & (lane < 2 * A_HEADS)
    fcum = jnp.dot(tri, jnp.where(is_f, gates, 0.0), preferred_element_type=F32,
                   precision=lax.Precision.HIGHEST)
    cols = jnp.where(is_f, fcum, gates)
    eye = jnp.where(r == s, 1.0, 0.0).astype(F32)
    rows = lax.dot_general(eye, cols, NT, preferred_element_type=F32,
                           precision=lax.Precision.HIGHEST)
    half = lax.broadcasted_iota(jnp.int32, (1, LANES), 1) // A_DK

    for h in range(A_HEADS):
        pair = h // 2
        in_head = half == (h % 2)
        f_col = cols[:, A_HEADS + h:A_HEADS + h + 1]
        ig_col = cols[:, h:h + 1]
        f_row = rows[A_HEADS + h:A_HEADS + h + 1, :]
        ig_row = rows[h:h + 1, :]
        m_prev = m_scr[h:h + 1, 0:1]

        dmat = jnp.where(causal, f_col + (ig_row - f_row), MASKED)
        g_inter = f_col + m_prev
        m = jnp.maximum(g_inter, jnp.max(dmat, axis=-1, keepdims=True))
        w_intra = jnp.exp(dmat - m)
        w_inter = jnp.exp(g_inter - m)

        qp = q_ref[:, pair * LANES:(pair + 1) * LANES]
        kp = k_ref[:, pair * LANES:(pair + 1) * LANES]
        qh = jnp.where(in_head, qp, jnp.zeros_like(qp))
        vh = v_ref[:, h * HEAD_W:(h + 1) * HEAD_W]
        ct = c_scr[h]
        nh = n_scr[h:h + 1, :]

        sc = lax.dot_general(qh, kp, NT, preferred_element_type=F32) * w_intra
        num = (jnp.dot(sc.astype(BF16), vh, preferred_element_type=F32)
               + w_inter * jnp.dot(qh, ct.astype(BF16), preferred_element_type=F32))
        den = (jnp.sum(sc, axis=-1, keepdims=True)
               + w_inter * jnp.sum(qh.astype(F32) * nh, axis=-1, keepdims=True))
        hb = num / jnp.maximum(jnp.abs(den), jnp.exp(-m))
        y_ref[:, h * HEAD_W:(h + 1) * HEAD_W] = (
            hb * og_ref[:, h * HEAD_W:(h + 1) * HEAD_W].astype(F32)).astype(y_ref.dtype)

        b_last = f_col[L - 1:L, :]
        g_state = b_last - f_col + ig_col
        m_new = jnp.maximum(b_last + m_prev, jnp.max(g_state, axis=0, keepdims=True))
        w_s = jnp.exp(g_state - m_new)
        decay = jnp.exp(b_last + m_prev - m_new)
        kw = jnp.where(in_head, kp.astype(F32), 0.0) * w_s
        c_scr[h] = decay * ct + lax.dot_general(kw.astype(BF16), vh, TN, preferred_element_type=F32)
        n_scr[h:h + 1, :] = decay * nh + jnp.sum(kw, axis=0, keepdims=True)
        m_scr[h:h + 1, :] = jnp.broadcast_to(m_new, (1, LANES))

    @pl.when(chunk == pl.num_programs(1) - 1)
    def _():
        cfin_ref[...] = c_scr[...]
        nfin_ref[...] = n_scr[...]
        mfin_ref[...] = m_scr[...]


def _a_scan(q, k, v, og, gates, c0, n0, m0, nb, seq):
    L = SCAN_CHUNK
    nc = seq // L
    row = lambda c: pl.BlockSpec((L, c), lambda b, i: (b * nc + i, 0))
    d_v = v.shape[1]
    return pl.pallas_call(
        _a_scan_kernel,
        grid=(nb, nc),
        in_specs=[row(q.shape[1]), row(k.shape[1]), row(d_v), row(d_v), row(LANES),
                  _const_spec(c0.shape), _const_spec(n0.shape), _const_spec(m0.shape)],
        out_specs=[row(d_v), pl.BlockSpec(c0.shape, lambda b, i: (0, 0, 0)),
                   pl.BlockSpec(n0.shape, lambda b, i: (0, 0)), pl.BlockSpec(m0.shape, lambda b, i: (0, 0))],
        out_shape=[jax.ShapeDtypeStruct((nb * seq, d_v), BF16), jax.ShapeDtypeStruct(c0.shape, F32),
                   jax.ShapeDtypeStruct(n0.shape, F32), jax.ShapeDtypeStruct(m0.shape, F32)],
        scratch_shapes=[pltpu.VMEM(c0.shape, F32), pltpu.VMEM(n0.shape, F32), pltpu.VMEM(m0.shape, F32)],
        compiler_params=_params(2),
        name="mlstm_scan",
    )(q, k, v, og, gates, c0, n0, m0)


def _a_out_kernel(y_ref, w_ref, g_ref, h_ref, o_ref):
    y = jnp.dot(y_ref[...], w_ref[...], preferred_element_type=F32)
    o_ref[...] = h_ref[...] + _rmsnorm(y, g_ref[...])


def _a_out(y, w, g, h, tm):
    m, d = h.shape
    return pl.pallas_call(
        _a_out_kernel,
        grid=(m // tm,),
        in_specs=[pl.BlockSpec((tm, y.shape[1]), lambda i: (i, 0)), _const_spec(w.shape), _const_spec((1, d)),
                  pl.BlockSpec((tm, d), lambda i: (i, 0))],
        out_specs=pl.BlockSpec((tm, d), lambda i: (i, 0)),
        out_shape=jax.ShapeDtypeStruct((m, d), F32),
        compiler_params=_params(1),
        name="mlstm_out",
    )(y, w, g, h)


def _b_proj_kernel(h_ref, g_ref, w_ref, fb_ref, lbraw_ref, q_ref, k_ref, lf_ref, v_ref, sg_ref):
    d_f = B_HEADS * HEAD_W
    raw = lbraw_ref[...]
    e = jnp.exp(raw - jnp.max(raw, axis=0, keepdims=True))
    sm = e / jnp.sum(e, axis=0, keepdims=True)
    lb = (sm[0:1, :] + sm[1:2, :]) - sm[0:1, :]

    u = _rmsnorm(h_ref[...], g_ref[...]).astype(BF16)
    zq = jnp.dot(u, w_ref[:, :d_f], preferred_element_type=F32)
    q = (zq * _sigmoid(zq)).astype(BF16)
    fpre = jnp.dot(u, w_ref[:, d_f:2 * d_f], preferred_element_type=F32) + fb_ref[...]
    a = jnp.log(lb)
    b = jnp.log1p(-lb) + _log_sigmoid(fpre)
    hi = jnp.maximum(a, b)
    lf = hi + jnp.log1p(jnp.exp(jnp.minimum(a, b) - hi))
    k = ((1.0 - lb) * _sigmoid(-fpre)).astype(BF16)
    v = jnp.dot(u, w_ref[:, 2 * d_f:3 * d_f], preferred_element_type=F32).astype(BF16)
    g = jnp.dot(u, w_ref[:, 3 * d_f:], preferred_element_type=F32)
    sg_ref[...] = _sigmoid(g).astype(BF16)
    for h in range(B_HEADS):
        sl = slice(h * HEAD_W, (h + 1) * HEAD_W)
        q_ref[h] = q[:, sl]
        k_ref[h] = k[:, sl]
        lf_ref[h] = lf[:, sl]
        v_ref[h] = v[:, sl]


def _b_proj(h, g, w, fb, lbraw, tm):
    m, d = h.shape
    heads = pl.BlockSpec((B_HEADS, tm, HEAD_W), lambda i: (0, i, 0))
    hshape = lambda dt: jax.ShapeDtypeStruct((B_HEADS, m, HEAD_W), dt)
    return pl.pallas_call(
        _b_proj_kernel,
        grid=(m // tm,),
        in_specs=[pl.BlockSpec((tm, d), lambda i: (i, 0)), _const_spec((1, d)), _const_spec(w.shape),
                  _const_spec(fb.shape), _const_spec(lbraw.shape)],
        out_specs=[heads, heads, heads, heads, pl.BlockSpec((tm, B_HEADS * HEAD_W), lambda i: (i, 0))],
        out_shape=[hshape(BF16), hshape(BF16), hshape(F32), hshape(BF16),
                   jax.ShapeDtypeStruct((m, B_HEADS * HEAD_W), BF16)],
        compiler_params=_params(1),
        name="hgrn_proj",
    )(h, g, w, fb, lbraw)


def _b_scan_kernel(q_ref, k_ref, lf_ref, v_ref, s0_ref, o_ref, sfin_ref, s_scr):
    L = SCAN_CHUNK
    n_levels = int(math.log2(L))
    chunk = pl.program_id(1)

    @pl.when(chunk == 0)
    def _():
        s_scr[...] = s0_ref[...]

    def head(h, carry):
        q = q_ref[h].astype(F32)
        k = k_ref[h].astype(F32)
        v = v_ref[h]
        row = lax.broadcasted_iota(jnp.int32, (L, HEAD_W), 0)
        r = lax.broadcasted_iota(jnp.int32, (L, L), 0)
        s = lax.broadcasted_iota(jnp.int32, (L, L), 1)
        a = jnp.where(r == s, lax.dot_general(q.astype(BF16), k.astype(BF16), NT, preferred_element_type=F32), 0.0)
        pre = lf_ref[h]
        tot = pre
        for lev in range(n_levels):
            b = 1 << lev
            upper = (row & b) != 0
            e = jnp.exp(jnp.where(upper, pre, tot - pre))
            qt = jnp.where(upper, q * e, 0.0).astype(BF16)
            kt = jnp.where(upper, 0.0, k * e).astype(BF16)
            p = lax.dot_general(qt, kt, NT, preferred_element_type=F32)
            a = a + jnp.where((r >> (lev + 1)) == (s >> (lev + 1)), p, 0.0)
            below = pltpu.roll(tot, b, 0)
            above = pltpu.roll(tot, L - b, 0)
            pre = pre + jnp.where(upper, below, 0.0)
            tot = tot + jnp.where(upper, below, above)
        st = s_scr[h]
        o = (jnp.dot(a.astype(BF16), v, preferred_element_type=F32)
             + lax.dot_general((q * jnp.exp(pre)).astype(BF16), st.astype(BF16), NT, preferred_element_type=F32))
        o_ref[h] = o.astype(o_ref.dtype)
        kd = (k * jnp.exp(tot - pre)).astype(BF16)
        s_scr[h] = st * jnp.exp(tot[0:1, :]) + lax.dot_general(v, kd, TN, preferred_element_type=F32)
        return carry

    lax.fori_loop(0, B_HEADS, head, 0)

    @pl.when(chunk == pl.num_programs(1) - 1)
    def _():
        sfin_ref[...] = s_scr[...]


def _b_scan(q, k, lf, v, s0, nb, seq):
    L = SCAN_CHUNK
    nc = seq // L
    heads = pl.BlockSpec((B_HEADS, L, HEAD_W), lambda b, i: (0, b * nc + i, 0))
    return pl.pallas_call(
        _b_scan_kernel,
        grid=(nb, nc),
        in_specs=[heads, heads, heads, heads, _const_spec(s0.shape)],
        out_specs=[heads, pl.BlockSpec(s0.shape, lambda b, i: (0, 0, 0))],
        out_shape=[jax.ShapeDtypeStruct((B_HEADS, nb * seq, HEAD_W), BF16), jax.ShapeDtypeStruct(s0.shape, F32)],
        scratch_shapes=[pltpu.VMEM(s0.shape, F32)],
        compiler_params=_params(2),
        name="hgrn_scan",
    )(q, k, lf, v, s0)


def _b_out_kernel(o_ref, sg_ref, gn_ref, w_ref, g_ref, h_ref, out_ref):
    o = jnp.concatenate([o_ref[h].astype(F32) for h in range(B_HEADS)], axis=-1)
    y = _rmsnorm(o * sg_ref[...].astype(F32), gn_ref[...]).astype(BF16)
    z = jnp.dot(y, w_ref[...], preferred_element_type=F32)
    out_ref[...] = h_ref[...] + _rmsnorm(z, g_ref[...])


def _b_out(o, sg, gn, w, g, h, tm):
    m, d = h.shape
    return pl.pallas_call(
        _b_out_kernel,
        grid=(m // tm,),
        in_specs=[pl.BlockSpec((B_HEADS, tm, HEAD_W), lambda i: (0, i, 0)),
                  pl.BlockSpec((tm, sg.shape[1]), lambda i: (i, 0)), _const_spec(gn.shape), _const_spec(w.shape),
                  _const_spec((1, d)), pl.BlockSpec((tm, d), lambda i: (i, 0))],
        out_specs=pl.BlockSpec((tm, d), lambda i: (i, 0)),
        out_shape=jax.ShapeDtypeStruct((m, d), F32),
        compiler_params=_params(1),
        name="hgrn_out",
    )(o, sg, gn, w, g, h)


def _pad_rows(x, axis, n, value=0.0):
    pad = [(0, 0)] * x.ndim
    pad[axis] = (0, n - x.shape[axis])
    return jnp.pad(x, pad, constant_values=value)


def _trunk(h, p, state, nb, seq, tm):
    gn = p["gains"]
    scan_seq = max(seq, SCAN_CHUNK)
    short = scan_seq != seq

    h = _ffn(h, gn[0][0], gn[0][1], p["ffn_in"][0][0], p["ffn_out"][0][0], tm)
    q, k, v, og, gates, tail = _a_proj(h, gn[0][2], p["a_w"], p["a_wg"], p["a_cw"], p["a_cb"], p["a_gb"],
                                      state["tail"], tm, seq)
    if short:
        q, k, v, og = (_pad_rows(t, 0, scan_seq) for t in (q, k, v, og))
        lane = jnp.arange(LANES)[None, :]
        pad_gate = jnp.where(lane < A_HEADS, MASKED, 0.0).astype(F32)
        gates = jnp.concatenate([gates, jnp.broadcast_to(pad_gate, (scan_seq - seq, LANES))], axis=0)
    y, c_fin, n_fin, m_fin = _a_scan(q, k, v, og, gates, state["c"], state["n"], state["m"], nb, scan_seq)
    if short:
        y = y[:seq]
    h = _a_out(y, p["a_wo"], gn[0][3], h, tm)
    h = _ffn(h, gn[0][4], gn[0][5], p["ffn_in"][0][1], p["ffn_out"][0][1], tm)

    h = _ffn(h, gn[1][0], gn[1][1], p["ffn_in"][1][0], p["ffn_out"][1][0], tm)
    q, k, lf, v, sg = _b_proj(h, gn[1][2], p["b_w"], p["b_fb"], p["b_lbraw"], tm)
    if short:
        q, k, lf, v = (_pad_rows(t, 1, scan_seq) for t in (q, k, lf, v))
    o, s_fin = _b_scan(q, k, lf, v, state["s"], nb, scan_seq)
    if short:
        o = o[:, :seq]
    h = _b_out(o, sg, p["b_gn"], p["b_wo"], gn[1][3], h, tm)
    h = _ffn(h, gn[1][4], gn[1][5], p["ffn_in"][1][1], p["ffn_out"][1][1], tm)
    return h, {"tail": tail, "c": c_fin, "n": n_fin, "m": m_fin, "s": s_fin}


def kernel(x, meta_tokens, norm_gains, ffn_w_in, ffn_w_out, a_w_in, a_conv_w, a_conv_b, a_gate_b, a_w_out,
           b_w_in, b_f_bias, b_lb_raw, b_g_norm, b_w_out):
    bsz, seq, d = x.shape
    depth = norm_gains.shape[0]
    assert depth == 2 and a_w_in.shape[0] == 1 and b_w_in.shape[0] == 1
    d_ff = ffn_w_out.shape[2]
    n_ff = d_ff // FF_TILE
    d_qkvo = 2 * A_HEADS * A_DK + 2 * A_HEADS * HEAD_W

    def ffn_in(w):
        gate = w[:, :d_ff].reshape(d, n_ff, FF_TILE)
        up = w[:, d_ff:].reshape(d, n_ff, FF_TILE)
        return jnp.concatenate([gate, up], axis=-1).transpose(1, 0, 2).astype(BF16)

    p = {
        "gains": [[norm_gains[i, j][None, :] for j in range(6)] for i in range(depth)],
        "ffn_in": [[ffn_in(ffn_w_in[i, j]) for j in range(2)] for i in range(depth)],
        "ffn_out": [[ffn_w_out[i, j].reshape(n_ff, FF_TILE, d).astype(BF16) for j in range(2)] for i in range(depth)],
        "a_w": a_w_in[0][:, :d_qkvo].astype(BF16),
        "a_wg": _pad_rows(a_w_in[0][:, d_qkvo:], 1, LANES).astype(BF16),
        "a_cw": a_conv_w[0], "a_cb": a_conv_b[0][None, :],
        "a_gb": _pad_rows(a_gate_b[0][None, :], 1, LANES),
        "a_wo": a_w_out[0].astype(BF16),
        "b_w": b_w_in[0].astype(BF16), "b_fb": b_f_bias[0][None, :], "b_lbraw": b_lb_raw,
        "b_gn": b_g_norm[0][None, :], "b_wo": b_w_out[0].astype(BF16),
    }
    state0 = {
        "tail": jnp.zeros((SUBLANES, 2 * A_HEADS * A_DK), F32),
        "c": jnp.zeros((A_HEADS, LANES, HEAD_W), F32),
        "n": jnp.zeros((A_HEADS, LANES), F32),
        "m": jnp.zeros((A_HEADS, LANES), F32),
        "s": jnp.zeros((B_HEADS, HEAD_W, HEAD_W), F32),
    }
    _, state = _trunk(meta_tokens.astype(x.dtype), p, state0, 1, N_META, N_META)
    out, _ = _trunk(x.reshape(bsz * seq, d), p, state, bsz, seq, min(512, seq))
    return out.reshape(bsz, seq, d)
```

```python
import functools
import math

import jax
import jax.numpy as jnp
from jax import lax
from jax.experimental import pallas as pl
from jax.experimental.pallas import tpu as pltpu

LANES = 128
SUBLANES = 8
VMEM_LIMIT_BYTES = 56 * 1024 * 1024

N_META = 16
EPS = 1e-6
A_HEADS = 8
A_DK = 64
A_CONV = 4
B_HEADS = 8
HEAD_W = 128
SCAN_CHUNK = 128
FF_TILE = 256
MASKED = -1e30

F32 = jnp.float32
BF16 = jnp.bfloat16
NT = (((1,), (1,)), ((), ()))
TN = (((0,), (0,)), ((), ()))

assert SCAN_CHUNK == LANES and 2 * A_DK == LANES


def _rmsnorm(x, g):
    return x * lax.rsqrt(jnp.mean(x * x, axis=-1, keepdims=True) + EPS) * g


def _sigmoid(x):
    return 1.0 / (1.0 + jnp.exp(-x))


def _log_sigmoid(x):
    return jnp.minimum(x, 0.0) - jnp.log1p(jnp.exp(-jnp.abs(x)))


def _params(n_grid_axes):
    return pltpu.CompilerParams(dimension_semantics=("arbitrary",) * n_grid_axes,
                                vmem_limit_bytes=VMEM_LIMIT_BYTES)


def _const_spec(shape):
    zeros = (0,) * len(shape)
    return pl.BlockSpec(shape, lambda *_: zeros, pipeline_mode=pl.Buffered(1))


def _ffn_kernel(h_ref, gpre_ref, gpost_ref, win_ref, wout_ref, o_ref, u_ref, act_ref, *, tf):
    d_ff = wout_ref.shape[0]
    u_ref[...] = _rmsnorm(h_ref[...], gpre_ref[...]).astype(BF16)
    for lo in range(0, d_ff, tf):
        gate = jnp.dot(u_ref[...], win_ref[:, lo:lo + tf], preferred_element_type=F32)
        up = jnp.dot(u_ref[...], win_ref[:, d_ff + lo:d_ff + lo + tf], preferred_element_type=F32)
        act_ref[:, lo:lo + tf] = (gate * _sigmoid(gate) * up).astype(BF16)
    y = jnp.dot(act_ref[...], wout_ref[...], preferred_element_type=F32)
    o_ref[...] = h_ref[...] + 0.5 * _rmsnorm(y, gpost_ref[...])


def _ffn(h, g_pre, g_post, w_in, w_out, tm):
    m, d = h.shape
    d_ff = w_out.shape[0]
    row = pl.BlockSpec((tm, d), lambda i: (i, 0))
    return pl.pallas_call(
        functools.partial(_ffn_kernel, tf=FF_TILE),
        grid=(m // tm,),
        in_specs=[row, _const_spec((1, d)), _const_spec((1, d)),
                  _const_spec(w_in.shape), _const_spec(w_out.shape)],
        out_specs=row,
        out_shape=jax.ShapeDtypeStruct((m, d), F32),
        scratch_shapes=[pltpu.VMEM((tm, d), BF16), pltpu.VMEM((tm, d_ff), BF16)],
        compiler_params=_params(1),
        name="ffn",
    )(h, g_pre, g_post, w_in, w_out)


def _shift_rows(x, j, prev):
    r = pltpu.roll(x, j, 0)
    rp = pltpu.roll(prev, j, 0)
    row = lax.broadcasted_iota(jnp.int32, (SUBLANES, x.shape[1]), 0)
    head = jnp.where(row < j, rp, r[:SUBLANES])
    if x.shape[0] == SUBLANES:
        return head
    return jnp.concatenate([head, r[SUBLANES:]], axis=0)


def _a_proj_kernel(h_ref, g_ref, w_ref, wg_ref, cw_ref, cb_ref, gb_ref, tail0_ref,
                   q_ref, k_ref, v_ref, og_ref, gates_ref, tail_ref, tail_scr, *, tiles_per_seq):
    d_qk = A_HEADS * A_DK
    d_v = A_HEADS * HEAD_W

    @pl.when(pl.program_id(0) % tiles_per_seq == 0)
    def _():
        tail_scr[...] = tail0_ref[...]

    u = _rmsnorm(h_ref[...], g_ref[...]).astype(BF16)
    zqk = jnp.dot(u, w_ref[:, :2 * d_qk], preferred_element_type=F32)
    prev = tail_scr[...]
    cw = cw_ref[...]
    acc = cb_ref[...] + cw[A_CONV - 1:A_CONV, :] * zqk
    for j in range(1, A_CONV):
        acc = acc + cw[A_CONV - 1 - j:A_CONV - j, :] * _shift_rows(zqk, j, prev)
    tm = zqk.shape[0]
    tail_scr[...] = zqk[tm - SUBLANES:, :]
    tail_ref[...] = zqk[tm - SUBLANES:, :]
    qk = acc * _sigmoid(acc)
    q_ref[...] = (qk[:, :d_qk] * (A_DK ** -0.5)).astype(BF16)
    k_ref[...] = qk[:, d_qk:].astype(BF16)
    v_ref[...] = jnp.dot(u, w_ref[:, 2 * d_qk:2 * d_qk + d_v], preferred_element_type=F32).astype(BF16)
    og = jnp.dot(u, w_ref[:, 2 * d_qk + d_v:], preferred_element_type=F32)
    og_ref[...] = _sigmoid(og).astype(BF16)
    g = jnp.dot(u, wg_ref[...], preferred_element_type=F32) + gb_ref[...]
    gates_ref[:, :LANES] = g[:, :LANES]
    gates_ref[:, LANES:] = _log_sigmoid(g[:, LANES:])


def _a_proj(h, g, w, wg, cw, cb, gb, tail0, tm, seq):
    m, d = h.shape
    d_qk = A_HEADS * A_DK
    d_v = A_HEADS * HEAD_W
    row = lambda c: pl.BlockSpec((tm, c), lambda i: (i, 0))
    return pl.pallas_call(
        functools.partial(_a_proj_kernel, tiles_per_seq=seq // tm),
        grid=(m // tm,),
        in_specs=[row(d), _const_spec((1, d)), _const_spec(w.shape), _const_spec(wg.shape),
                  _const_spec(cw.shape), _const_spec(cb.shape), _const_spec(gb.shape), _const_spec(tail0.shape)],
        out_specs=[row(d_qk), row(d_qk), row(d_v), row(d_v), row(2 * LANES),
                   pl.BlockSpec((SUBLANES, 2 * d_qk), lambda i: (0, 0))],
        out_shape=[jax.ShapeDtypeStruct((m, d_qk), BF16), jax.ShapeDtypeStruct((m, d_qk), BF16),
                   jax.ShapeDtypeStruct((m, d_v), BF16), jax.ShapeDtypeStruct((m, d_v), BF16),
                   jax.ShapeDtypeStruct((m, 2 * LANES), F32), jax.ShapeDtypeStruct((SUBLANES, 2 * d_qk), F32)],
        scratch_shapes=[pltpu.VMEM((SUBLANES, 2 * d_qk), F32)],
        compiler_params=_params(1),
        name="mlstm_proj",
    )(h, g, w, wg, cw, cb, gb, tail0)


def _a_scan_kernel(q_ref, k_ref, v_ref, og_ref, gates_ref, c0_ref, m0_ref,
                   y_ref, cfin_ref, mfin_ref, c_scr, m_scr):
    L = SCAN_CHUNK
    chunk = pl.program_id(1)

    @pl.when(chunk == 0)
    def _():
        c_scr[...] = c0_ref[...]
        m_scr[...] = m0_ref[...]

    r = lax.broadcasted_iota(jnp.int32, (L, L), 0)
    s = lax.broadcasted_iota(jnp.int32, (L, L), 1)
    causal = s <= r
    tri = jnp.where(causal, 1.0, 0.0).astype(F32)
    fcum = jnp.dot(tri, gates_ref[:, LANES:], preferred_element_type=F32, precision=lax.Precision.HIGHEST)
    a = gates_ref[:, :LANES] - fcum
    row = lax.broadcasted_iota(jnp.int32, (L, LANES), 0)
    run = a
    for lev in range(int(math.log2(L))):
        b = 1 << lev
        run = jnp.maximum(run, jnp.where(row >= b, pltpu.roll(run, b, 0), MASKED))
    m_prev = m_scr[...]
    rmax = jnp.maximum(run, m_prev)
    mstab = fcum + rmax
    r_last = rmax[L - 1:L, :]
    decay = jnp.exp(m_prev - r_last)
    a_t = a.T
    ws_t = jnp.exp(a - r_last).T
    m_scr[...] = mstab[L - 1:L, :]

    key_half = lax.broadcasted_iota(jnp.int32, (LANES, 1), 0) // A_DK
    lane_half = lax.broadcasted_iota(jnp.int32, (1, LANES), 1) // A_DK
    ones = jnp.ones((L, HEAD_W), BF16)
    eye = jnp.where(r == s, 1.0, 0.0).astype(BF16)

    for pair in range(A_HEADS // 2):
        qp = q_ref[:, pair * LANES:(pair + 1) * LANES]
        kp = k_ref[:, pair * LANES:(pair + 1) * LANES]
        zero = jnp.zeros_like(kp)
        k_blocks = jnp.concatenate([jnp.where(lane_half == 0, kp, zero),
                                    jnp.where(lane_half == 1, kp, zero)], axis=0)
        scores = lax.dot_general(qp, k_blocks, NT, preferred_element_type=F32)
        k_t = lax.dot_general(eye, kp, NT, preferred_element_type=F32)
        for j in range(2):
            h = 2 * pair + j
            rb = jnp.broadcast_to(rmax[:, h:h + 1], (L, L))
            w_intra = jnp.exp(jnp.where(causal, a_t[h:h + 1, :] - rb, MASKED))
            w_inter = jnp.exp(m_prev[:, h:h + 1] - rb)
            sc = scores[:, j * L:(j + 1) * L] * w_intra
            q_in = jnp.where(lane_half == j, qp.astype(F32) * w_inter, 0.0)
            v_aug = jnp.concatenate([v_ref[:, h * HEAD_W:(h + 1) * HEAD_W], ones], axis=1)
            c_aug = c_scr[h]
            lhs = jnp.concatenate([sc.astype(BF16), q_in.astype(BF16)], axis=1)
            rhs = jnp.concatenate([v_aug, c_aug.astype(BF16)], axis=0)
            nd = jnp.dot(lhs, rhs, preferred_element_type=F32)
            mb = jnp.broadcast_to(mstab[:, h:h + 1], (L, HEAD_W))
            hb = nd[:, :HEAD_W] / jnp.maximum(jnp.abs(nd[:, HEAD_W:]), jnp.exp(-mb))
            y_ref[:, h * HEAD_W:(h + 1) * HEAD_W] = (
                hb * og_ref[:, h * HEAD_W:(h + 1) * HEAD_W].astype(F32)).astype(y_ref.dtype)
            kw_t = jnp.where(key_half == j, k_t, 0.0) * ws_t[h:h + 1, :]
            c_scr[h] = decay[:, h:h + 1] * c_aug + jnp.dot(kw_t.astype(BF16), v_aug, preferred_element_type=F32)

    @pl.when(chunk == pl.num_programs(1) - 1)
    def _():
        cfin_ref[...] = c_scr[...]
        mfin_ref[...] = m_scr[...]


def _a_scan(q, k, v, og, gates, c0, m0, nb, seq):
    L = SCAN_CHUNK
    nc = seq // L
    row = lambda c: pl.BlockSpec((L, c), lambda b, i: (b * nc + i, 0))
    d_v = v.shape[1]
    return pl.pallas_call(
        _a_scan_kernel,
        grid=(nb, nc),
        in_specs=[row(q.shape[1]), row(k.shape[1]), row(d_v), row(d_v), row(2 * LANES),
                  _const_spec(c0.shape), _const_spec(m0.shape)],
        out_specs=[row(d_v), pl.BlockSpec(c0.shape, lambda b, i: (0, 0, 0)),
                   pl.BlockSpec(m0.shape, lambda b, i: (0, 0))],
        out_shape=[jax.ShapeDtypeStruct((nb * seq, d_v), BF16), jax.ShapeDtypeStruct(c0.shape, F32),
                   jax.ShapeDtypeStruct(m0.shape, F32)],
        scratch_shapes=[pltpu.VMEM(c0.shape, F32), pltpu.VMEM(m0.shape, F32)],
        compiler_params=_params(2),
        name="mlstm_scan",
    )(q, k, v, og, gates, c0, m0)


def _a_out_kernel(y_ref, w_ref, g_ref, h_ref, o_ref):
    y = jnp.dot(y_ref[...], w_ref[...], preferred_element_type=F32)
    o_ref[...] = h_ref[...] + _rmsnorm(y, g_ref[...])


def _a_out(y, w, g, h, tm):
    m, d = h.shape
    return pl.pallas_call(
        _a_out_kernel,
        grid=(m // tm,),
        in_specs=[pl.BlockSpec((tm, y.shape[1]), lambda i: (i, 0)), _const_spec(w.shape), _const_spec((1, d)),
                  pl.BlockSpec((tm, d), lambda i: (i, 0))],
        out_specs=pl.BlockSpec((tm, d), lambda i: (i, 0)),
        out_shape=jax.ShapeDtypeStruct((m, d), F32),
        compiler_params=_params(1),
        name="mlstm_out",
    )(y, w, g, h)


def _b_proj_kernel(h_ref, g_ref, w_ref, fb_ref, lbraw_ref, q_ref, k_ref, f_ref, v_ref, sg_ref):
    d_f = B_HEADS * HEAD_W
    raw = lbraw_ref[...]
    e = jnp.exp(raw - jnp.max(raw, axis=0, keepdims=True))
    sm = e / jnp.sum(e, axis=0, keepdims=True)
    lb = (sm[0:1, :] + sm[1:2, :]) - sm[0:1, :]

    u = _rmsnorm(h_ref[...], g_ref[...]).astype(BF16)
    zq = jnp.dot(u, w_ref[:, :d_f], preferred_element_type=F32)
    q = (zq * _sigmoid(zq)).astype(BF16)
    fpre = jnp.dot(u, w_ref[:, d_f:2 * d_f], preferred_element_type=F32) + fb_ref[...]
    en = jnp.exp(-jnp.abs(fpre))
    big = 1.0 / (1.0 + en)
    small = en * big
    pos = fpre >= 0.0
    f = lb + (1.0 - lb) * jnp.where(pos, big, small)
    k = ((1.0 - lb) * jnp.where(pos, small, big)).astype(BF16)
    v = jnp.dot(u, w_ref[:, 2 * d_f:3 * d_f], preferred_element_type=F32).astype(BF16)
    g = jnp.dot(u, w_ref[:, 3 * d_f:], preferred_element_type=F32)
    sg_ref[...] = _sigmoid(g).astype(BF16)
    for h in range(B_HEADS):
        sl = slice(h * HEAD_W, (h + 1) * HEAD_W)
        q_ref[h] = q[:, sl]
        k_ref[h] = k[:, sl]
        f_ref[h] = f[:, sl]
        v_ref[h] = v[:, sl]


def _b_proj(h, g, w, fb, lbraw, tm):
    m, d = h.shape
    heads = pl.BlockSpec((B_HEADS, tm, HEAD_W), lambda i: (0, i, 0))
    hshape = lambda dt: jax.ShapeDtypeStruct((B_HEADS, m, HEAD_W), dt)
    return pl.pallas_call(
        _b_proj_kernel,
        grid=(m // tm,),
        in_specs=[pl.BlockSpec((tm, d), lambda i: (i, 0)), _const_spec((1, d)), _const_spec(w.shape),
                  _const_spec(fb.shape), _const_spec(lbraw.shape)],
        out_specs=[heads, heads, heads, heads, pl.BlockSpec((tm, B_HEADS * HEAD_W), lambda i: (i, 0))],
        out_shape=[hshape(BF16), hshape(BF16), hshape(F32), hshape(BF16),
                   jax.ShapeDtypeStruct((m, B_HEADS * HEAD_W), BF16)],
        compiler_params=_params(1),
        name="hgrn_proj",
    )(h, g, w, fb, lbraw)


def _lower_halves(x, b):
    return jnp.concatenate([x[i:i + b] for i in range(0, x.shape[0], 2 * b)], axis=0)


def _upper_halves(x, b):
    return jnp.concatenate([x[i + b:i + 2 * b] for i in range(0, x.shape[0], 2 * b)], axis=0)


def _interleave_halves(lo, up, b):
    pieces = []
    for i in range(0, lo.shape[0], b):
        pieces += [lo[i:i + b], up[i:i + b]]
    return jnp.concatenate(pieces, axis=0)


def _b_scan_kernel(q_ref, k_ref, f_ref, v_ref, s0_ref, o_ref, sfin_ref, s_scr, mask_scr):
    L = SCAN_CHUNK
    n_levels = int(math.log2(L))
    chunk = pl.program_id(1)

    @pl.when(chunk == 0)
    def _():
        s_scr[...] = s0_ref[...]

    @pl.when((chunk == 0) & (pl.program_id(0) == 0))
    def _():
        r = lax.broadcasted_iota(jnp.int32, (L, L), 0)
        s = lax.broadcasted_iota(jnp.int32, (L, L), 1)
        mask_scr[0] = jnp.where(r == s, 1.0, 0.0)
        for lev in range(n_levels - 1):
            mask_scr[lev + 1] = jnp.where((r >> (lev + 1)) == (s >> (lev + 1)), 1.0, 0.0)

    def head(h, carry):
        qb = q_ref[h]
        kb = k_ref[h]
        q = qb.astype(F32)
        k = kb.astype(F32)
        v = v_ref[h]
        row = lax.broadcasted_iota(jnp.int32, (L, HEAD_W), 0)
        a = lax.dot_general(qb, kb, NT, preferred_element_type=F32) * mask_scr[0]
        tot = f_ref[h]
        pre = tot
        post = jnp.ones_like(tot)
        for lev in range(n_levels):
            b = 1 << lev
            if b >= SUBLANES:
                tot_lo, tot_up = _lower_halves(tot, b), _upper_halves(tot, b)
                pre_lo, pre_up = _lower_halves(pre, b), _upper_halves(pre, b)
                post_lo, post_up = _lower_halves(post, b), _upper_halves(post, b)
                zeros = jnp.zeros_like(tot_lo)
                qt = _interleave_halves(zeros, _upper_halves(q, b) * pre_up, b)
                kt = _interleave_halves(_lower_halves(k, b) * post_lo, zeros, b)
                pre = _interleave_halves(pre_lo, pre_up * tot_lo, b)
                post = _interleave_halves(post_lo * tot_up, post_up, b)
                both = tot_lo * tot_up
                tot = _interleave_halves(both, both, b)
            else:
                upper = (row & b) != 0
                below = pltpu.roll(tot, b, 0)
                above = pltpu.roll(tot, L - b, 0)
                qt = jnp.where(upper, q * pre, 0.0)
                kt = jnp.where(upper, 0.0, k * post)
                pre = pre * jnp.where(upper, below, 1.0)
                post = post * jnp.where(upper, 1.0, above)
                tot = tot * jnp.where(upper, below, above)
            p = lax.dot_general(qt.astype(BF16), kt.astype(BF16), NT, preferred_element_type=F32)
            a = a + (p if lev == n_levels - 1 else p * mask_scr[lev + 1])
        st = s_scr[h]
        o = (jnp.dot(a.astype(BF16), v, preferred_element_type=F32)
             + lax.dot_general((q * pre).astype(BF16), st.astype(BF16), NT, preferred_element_type=F32))
        o_ref[h] = o.astype(o_ref.dtype)
        kd = (k * post).astype(BF16)
        s_scr[h] = st * tot[0:1, :] + lax.dot_general(v, kd, TN, preferred_element_type=F32)
        return carry

    lax.fori_loop(0, B_HEADS, head, 0, unroll=True)

    @pl.when(chunk == pl.num_programs(1) - 1)
    def _():
        sfin_ref[...] = s_scr[...]


def _b_scan(q, k, f, v, s0, nb, seq):
    L = SCAN_CHUNK
    nc = seq // L
    heads = pl.BlockSpec((B_HEADS, L, HEAD_W), lambda b, i: (0, b * nc + i, 0))
    n_masks = int(math.log2(L))
    return pl.pallas_call(
        _b_scan_kernel,
        grid=(nb, nc),
        in_specs=[heads, heads, heads, heads, _const_spec(s0.shape)],
        out_specs=[heads, pl.BlockSpec(s0.shape, lambda b, i: (0, 0, 0))],
        out_shape=[jax.ShapeDtypeStruct((B_HEADS, nb * seq, HEAD_W), BF16), jax.ShapeDtypeStruct(s0.shape, F32)],
        scratch_shapes=[pltpu.VMEM(s0.shape, F32), pltpu.VMEM((n_masks, L, L), F32)],
        compiler_params=_params(2),
        name="hgrn_scan",
    )(q, k, f, v, s0)


def _b_out_kernel(o_ref, sg_ref, gn_ref, w_ref, g_ref, h_ref, out_ref):
    o = jnp.concatenate([o_ref[h].astype(F32) for h in range(B_HEADS)], axis=-1)
    y = _rmsnorm(o * sg_ref[...].astype(F32), gn_ref[...]).astype(BF16)
    z = jnp.dot(y, w_ref[...], preferred_element_type=F32)
    out_ref[...] = h_ref[...] + _rmsnorm(z, g_ref[...])


def _b_out(o, sg, gn, w, g, h, tm):
    m, d = h.shape
    return pl.pallas_call(
        _b_out_kernel,
        grid=(m // tm,),
        in_specs=[pl.BlockSpec((B_HEADS, tm, HEAD_W), lambda i: (0, i, 0)),
                  pl.BlockSpec((tm, sg.shape[1]), lambda i: (i, 0)), _const_spec(gn.shape), _const_spec(w.shape),
                  _const_spec((1, d)), pl.BlockSpec((tm, d), lambda i: (i, 0))],
        out_specs=pl.BlockSpec((tm, d), lambda i: (i, 0)),
        out_shape=jax.ShapeDtypeStruct((m, d), F32),
        compiler_params=_params(1),
        name="hgrn_out",
    )(o, sg, gn, w, g, h)


def _pad_rows(x, axis, n, value=0.0):
    pad = [(0, 0)] * x.ndim
    pad[axis] = (0, n - x.shape[axis])
    return jnp.pad(x, pad, constant_values=value)


def _trunk(h, p, state, nb, seq, tm):
    gn = p["gains"]
    scan_seq = max(seq, SCAN_CHUNK)
    short = scan_seq != seq

    h = _ffn(h, gn[0][0], gn[0][1], p["ffn_in"][0][0], p["ffn_out"][0][0], tm)
    q, k, v, og, gates, tail = _a_proj(h, gn[0][2], p["a_w"], p["a_wg"], p["a_cw"], p["a_cb"], p["a_gb"],
                                      state["tail"], tm, seq)
    if short:
        q, k, v, og = (_pad_rows(t, 0, scan_seq) for t in (q, k, v, og))
        lane = jnp.arange(2 * LANES)[None, :]
        pad_gate = jnp.where(lane < LANES, MASKED, 0.0).astype(F32)
        gates = jnp.concatenate([gates, jnp.broadcast_to(pad_gate, (scan_seq - seq, 2 * LANES))], axis=0)
    y, c_fin, m_fin = _a_scan(q, k, v, og, gates, state["c"], state["m"], nb, scan_seq)
    if short:
        y = y[:seq]
    h = _a_out(y, p["a_wo"], gn[0][3], h, tm)
    h = _ffn(h, gn[0][4], gn[0][5], p["ffn_in"][0][1], p["ffn_out"][0][1], tm)

    h = _ffn(h, gn[1][0], gn[1][1], p["ffn_in"][1][0], p["ffn_out"][1][0], tm)
    q, k, f, v, sg = _b_proj(h, gn[1][2], p["b_w"], p["b_fb"], p["b_lbraw"], tm)
    if short:
        q, k, v = (_pad_rows(t, 1, scan_seq) for t in (q, k, v))
        f = _pad_rows(f, 1, scan_seq, 1.0)
    o, s_fin = _b_scan(q, k, f, v, state["s"], nb, scan_seq)
    if short:
        o = o[:, :seq]
    h = _b_out(o, sg, p["b_gn"], p["b_wo"], gn[1][3], h, tm)
    h = _ffn(h, gn[1][4], gn[1][5], p["ffn_in"][1][1], p["ffn_out"][1][1], tm)
    return h, {"tail": tail, "c": c_fin, "m": m_fin, "s": s_fin}


def kernel(x, meta_tokens, norm_gains, ffn_w_in, ffn_w_out, a_w_in, a_conv_w, a_conv_b, a_gate_b, a_w_out,
           b_w_in, b_f_bias, b_lb_raw, b_g_norm, b_w_out):
    bsz, seq, d = x.shape
    depth = norm_gains.shape[0]
    assert depth == 2 and a_w_in.shape[0] == 1 and b_w_in.shape[0] == 1
    assert ffn_w_out.shape[2] % FF_TILE == 0
    d_qkvo = 2 * A_HEADS * A_DK + 2 * A_HEADS * HEAD_W
    ffn_in_bf16 = ffn_w_in.astype(BF16)
    ffn_out_bf16 = ffn_w_out.astype(BF16)

    def gate_lanes(t):
        z = jnp.zeros(t.shape[:-1] + (LANES - A_HEADS,), t.dtype)
        return jnp.concatenate([t[..., :A_HEADS], z, t[..., A_HEADS:], z], axis=-1)

    p = {
        "gains": [[norm_gains[i, j][None, :] for j in range(6)] for i in range(depth)],
        "ffn_in": [[ffn_in_bf16[i, j] for j in range(2)] for i in range(depth)],
        "ffn_out": [[ffn_out_bf16[i, j] for j in range(2)] for i in range(depth)],
        "a_w": a_w_in[0][:, :d_qkvo].astype(BF16),
        "a_wg": gate_lanes(a_w_in[0][:, d_qkvo:]).astype(BF16),
        "a_cw": a_conv_w[0], "a_cb": a_conv_b[0][None, :],
        "a_gb": gate_lanes(a_gate_b[0][None, :]),
        "a_wo": a_w_out[0].astype(BF16),
        "b_w": b_w_in[0].astype(BF16), "b_fb": b_f_bias[0][None, :], "b_lbraw": b_lb_raw,
        "b_gn": b_g_norm[0][None, :], "b_wo": b_w_out[0].astype(BF16),
    }
    state0 = {
        "tail": jnp.zeros((SUBLANES, 2 * A_HEADS * A_DK), F32),
        "c": jnp.zeros((A_HEADS, LANES, 2 * HEAD_W), F32),
        "m": jnp.zeros((1, LANES), F32),
        "s": jnp.zeros((B_HEADS, HEAD_W, HEAD_W), F32),
    }
    _, state = _trunk(meta_tokens.astype(x.dtype), p, state0, 1, N_META, N_META)
    out, _ = _trunk(x.reshape(bsz * seq, d), p, state, bsz, seq, min(512, seq))
    return out.reshape(bsz, seq, d)
```

```python
import functools
import math

import jax
import jax.numpy as jnp
from jax import lax
from jax.experimental import pallas as pl
from jax.experimental.pallas import tpu as pltpu

LANES = 128
SUBLANES = 8
VMEM_LIMIT_BYTES = 56 * 1024 * 1024

N_META = 16
EPS = 1e-6
A_HEADS = 8
A_DK = 64
A_CONV = 4
B_HEADS = 8
HEAD_W = 128
SCAN_CHUNK = 128
FF_TILE = 256
FFN_ROWS = 512
FFN_SUB_ROWS = 256
FFN_OVERLAP_AT = 1
PROJ_ROWS = 512
PROJ_SUB_ROWS = 256
PROJ_TILE = 256
MASKED = -1e30

F32 = jnp.float32
BF16 = jnp.bfloat16
NT = (((1,), (1,)), ((), ()))
TN = (((0,), (0,)), ((), ()))

assert SCAN_CHUNK == LANES and 2 * A_DK == LANES and PROJ_TILE == 2 * HEAD_W


def _rmsnorm(x, g):
    return x * lax.rsqrt(jnp.mean(x * x, axis=-1, keepdims=True) + EPS) * g


def _sigmoid(x):
    return 1.0 / (1.0 + jnp.exp(-x))


def _log_sigmoid(x):
    return jnp.minimum(x, 0.0) - jnp.log1p(jnp.exp(-jnp.abs(x)))


def _params(n_grid_axes):
    return pltpu.CompilerParams(dimension_semantics=("arbitrary",) * n_grid_axes,
                                vmem_limit_bytes=VMEM_LIMIT_BYTES)


def _const_spec(shape):
    zeros = (0,) * len(shape)
    return pl.BlockSpec(shape, lambda *_: zeros, pipeline_mode=pl.Buffered(1))


def _sub_tiles(n_rows, sub):
    return [slice(r0, r0 + sub) for r0 in range(0, n_rows, sub)]


def _ffn_kernel(*refs, mixer, sub):
    n_mix = {None: 0, "a": 3, "b": 5}[mixer]
    h_ref = refs[0]
    mix_refs = refs[1:1 + n_mix]
    gpre_ref, gpost_ref, win_ref, wout_ref, o_ref, u_ref, act_ref, y_scr = refs[1 + n_mix:]
    d_ff = wout_ref.shape[0]
    tiles = _sub_tiles(h_ref.shape[0], sub)

    def ffn_input(rows):
        return h_ref[rows, :] if mixer is None else o_ref[rows, :]

    def prologue(rows):
        if mixer == "a":
            y_ref, wo_ref, gmix_ref = mix_refs
            mix = y_ref[rows, :]
        elif mixer == "b":
            o8_ref, sg_ref, gn_ref, wo_ref, gmix_ref = mix_refs
            o = jnp.concatenate([o8_ref[hd, rows, :].astype(F32) for hd in range(B_HEADS)], axis=-1)
            mix = _rmsnorm(o * sg_ref[rows, :].astype(F32), gn_ref[...]).astype(BF16)
        if mixer is not None:
            z = jnp.dot(mix, wo_ref[...], preferred_element_type=F32)
            o_ref[rows, :] = h_ref[rows, :] + _rmsnorm(z, gmix_ref[...])
        u_ref[rows, :] = _rmsnorm(ffn_input(rows), gpre_ref[...]).astype(BF16)

    def activation(rows, lo):
        gate = jnp.dot(u_ref[rows, :], win_ref[:, lo:lo + FF_TILE], preferred_element_type=F32)
        up = jnp.dot(u_ref[rows, :], win_ref[:, d_ff + lo:d_ff + lo + FF_TILE], preferred_element_type=F32)
        act_ref[rows, lo:lo + FF_TILE] = (gate * _sigmoid(gate) * up).astype(BF16)

    def epilogue(rows):
        o_ref[rows, :] = ffn_input(rows) + 0.5 * _rmsnorm(y_scr[rows, :], gpost_ref[...])

    prologue(tiles[0])
    for t, rows in enumerate(tiles):
        for n, lo in enumerate(range(0, d_ff, FF_TILE)):
            activation(rows, lo)
            if n == FFN_OVERLAP_AT:
                if t + 1 < len(tiles):
                    prologue(tiles[t + 1])
                if t > 0:
                    epilogue(tiles[t - 1])
        y_scr[rows, :] = jnp.dot(act_ref[rows, :], wout_ref[...], preferred_element_type=F32)
    epilogue(tiles[-1])


def _ffn(h, g_pre, g_post, w_in, w_out, mixer=None):
    m, d = h.shape
    d_ff = w_out.shape[0]
    tm = min(m, FFN_ROWS)
    row = lambda c: pl.BlockSpec((tm, c), lambda i: (i, 0))
    vec = _const_spec((1, d))
    if mixer is None:
        kind, name = None, "ffn"
        args, specs = [], []
    elif mixer[0] == "a":
        kind, name = "a", "mlstm_out_ffn"
        _, y, w_o, g_mix = mixer
        args = [y, w_o, g_mix]
        specs = [row(y.shape[1]), _const_spec(w_o.shape), vec]
    else:
        kind, name = "b", "hgrn_out_ffn"
        _, o8, sg, g_norm, w_o, g_mix = mixer
        args = [o8, sg, g_norm, w_o, g_mix]
        specs = [pl.BlockSpec((B_HEADS, tm, HEAD_W), lambda i: (0, i, 0)), row(sg.shape[1]),
                 _const_spec(g_norm.shape), _const_spec(w_o.shape), vec]
    return pl.pallas_call(
        functools.partial(_ffn_kernel, mixer=kind, sub=min(tm, FFN_SUB_ROWS)),
        grid=(m // tm,),
        in_specs=[row(d)] + specs + [vec, vec, _const_spec(w_in.shape), _const_spec(w_out.shape)],
        out_specs=row(d),
        out_shape=jax.ShapeDtypeStruct((m, d), F32),
        scratch_shapes=[pltpu.VMEM((tm, d), BF16), pltpu.VMEM((tm, d_ff), BF16), pltpu.VMEM((tm, d), F32)],
        compiler_params=_params(1),
        name=name,
    )(h, *args, g_pre, g_post, w_in, w_out)


def _a_proj_kernel(h_ref, g_ref, w_ref, wg_ref, cw_ref, cb_ref, gb_ref, tail0_ref,
                   q_ref, k_ref, v_ref, og_ref, gates_ref, tail_ref, z_scr, u_ref, *, tiles_per_seq, sub):
    d_qk = A_HEADS * A_DK
    d_v = A_HEADS * HEAD_W

    @pl.when(pl.program_id(0) % tiles_per_seq == 0)
    def _():
        z_scr[:SUBLANES, :] = tail0_ref[...]

    for rows in _sub_tiles(h_ref.shape[0], sub):
        u_ref[rows, :] = _rmsnorm(h_ref[rows, :], g_ref[...]).astype(BF16)
        for c in range(0, 2 * d_qk, PROJ_TILE):
            cols = slice(c, c + PROJ_TILE)
            z = jnp.dot(u_ref[rows, :], w_ref[:, cols], preferred_element_type=F32)
            z_scr[SUBLANES:, cols] = z
            acc = cb_ref[:, cols] + cw_ref[A_CONV - 1:A_CONV, cols] * z
            for j in range(1, A_CONV):
                acc = acc + cw_ref[A_CONV - 1 - j:A_CONV - j, cols] * z_scr[SUBLANES - j:SUBLANES - j + sub, cols]
            z_scr[:SUBLANES, cols] = z[sub - SUBLANES:, :]
            act = acc * _sigmoid(acc)
            if c < d_qk:
                q_ref[rows, cols] = (act * (A_DK ** -0.5)).astype(BF16)
            else:
                k_ref[rows, c - d_qk:c - d_qk + PROJ_TILE] = act.astype(BF16)
            v_ref[rows, cols] = jnp.dot(u_ref[rows, :], w_ref[:, 2 * d_qk + c:2 * d_qk + c + PROJ_TILE],
                                        preferred_element_type=F32).astype(BF16)
            og = jnp.dot(u_ref[rows, :], w_ref[:, 2 * d_qk + d_v + c:2 * d_qk + d_v + c + PROJ_TILE],
                         preferred_element_type=F32)
            og_ref[rows, cols] = _sigmoid(og).astype(BF16)
        g = jnp.dot(u_ref[rows, :], wg_ref[...], preferred_element_type=F32) + gb_ref[...]
        gates_ref[rows, :LANES] = g[:, :LANES]
        gates_ref[rows, LANES:] = _log_sigmoid(g[:, LANES:])
    tail_ref[...] = z_scr[:SUBLANES, :]


def _a_proj(h, g, w, wg, cw, cb, gb, tail0, seq):
    m, d = h.shape
    tm = min(seq, PROJ_ROWS)
    sub = min(tm, PROJ_SUB_ROWS)
    d_qk = A_HEADS * A_DK
    d_v = A_HEADS * HEAD_W
    assert d_v == 2 * d_qk
    row = lambda c: pl.BlockSpec((tm, c), lambda i: (i, 0))
    return pl.pallas_call(
        functools.partial(_a_proj_kernel, tiles_per_seq=seq // tm, sub=sub),
        grid=(m // tm,),
        in_specs=[row(d), _const_spec((1, d)), _const_spec(w.shape), _const_spec(wg.shape),
                  _const_spec(cw.shape), _const_spec(cb.shape), _const_spec(gb.shape), _const_spec(tail0.shape)],
        out_specs=[row(d_qk), row(d_qk), row(d_v), row(d_v), row(2 * LANES),
                   pl.BlockSpec((SUBLANES, 2 * d_qk), lambda i: (0, 0))],
        out_shape=[jax.ShapeDtypeStruct((m, d_qk), BF16), jax.ShapeDtypeStruct((m, d_qk), BF16),
                   jax.ShapeDtypeStruct((m, d_v), BF16), jax.ShapeDtypeStruct((m, d_v), BF16),
                   jax.ShapeDtypeStruct((m, 2 * LANES), F32), jax.ShapeDtypeStruct((SUBLANES, 2 * d_qk), F32)],
        scratch_shapes=[pltpu.VMEM((SUBLANES + sub, 2 * d_qk), F32), pltpu.VMEM((tm, d), BF16)],
        compiler_params=_params(1),
        name="mlstm_proj",
    )(h, g, w, wg, cw, cb, gb, tail0)


def _a_scan_kernel(q_ref, k_ref, v_ref, og_ref, gates_ref, c0_ref, m0_ref,
                   y_ref, cfin_ref, mfin_ref, c_scr, m_scr):
    L = SCAN_CHUNK
    chunk = pl.program_id(1)

    @pl.when(chunk == 0)
    def _():
        c_scr[...] = c0_ref[...]
        m_scr[...] = m0_ref[...]

    r = lax.broadcasted_iota(jnp.int32, (L, L), 0)
    s = lax.broadcasted_iota(jnp.int32, (L, L), 1)
    causal = s <= r
    tri = jnp.where(causal, 1.0, 0.0).astype(F32)
    fcum = jnp.dot(tri, gates_ref[:, LANES:], preferred_element_type=F32, precision=lax.Precision.HIGHEST)
    a = gates_ref[:, :LANES] - fcum
    row = lax.broadcasted_iota(jnp.int32, (L, LANES), 0)
    run = a
    for lev in range(int(math.log2(L))):
        b = 1 << lev
        run = jnp.maximum(run, jnp.where(row >= b, pltpu.roll(run, b, 0), MASKED))
    m_prev = m_scr[...]
    rmax = jnp.maximum(run, m_prev)
    mstab = fcum + rmax
    r_last = rmax[L - 1:L, :]
    decay = jnp.exp(m_prev - r_last)
    a_t = a.T
    ws_t = jnp.exp(a - r_last).T
    m_scr[...] = mstab[L - 1:L, :]

    key_half = lax.broadcasted_iota(jnp.int32, (LANES, 1), 0) // A_DK
    lane_half = lax.broadcasted_iota(jnp.int32, (1, LANES), 1) // A_DK
    ones = jnp.ones((L, HEAD_W), BF16)
    eye = jnp.where(r == s, 1.0, 0.0).astype(BF16)

    for pair in range(A_HEADS // 2):
        qp = q_ref[:, pair * LANES:(pair + 1) * LANES]
        kp = k_ref[:, pair * LANES:(pair + 1) * LANES]
        zero = jnp.zeros_like(kp)
        k_blocks = jnp.concatenate([jnp.where(lane_half == 0, kp, zero),
                                    jnp.where(lane_half == 1, kp, zero)], axis=0)
        scores = lax.dot_general(qp, k_blocks, NT, preferred_element_type=F32)
        k_t = lax.dot_general(eye, kp, NT, preferred_element_type=F32)
        for j in range(2):
            h = 2 * pair + j
            rb = jnp.broadcast_to(rmax[:, h:h + 1], (L, L))
            w_intra = jnp.exp(jnp.where(causal, a_t[h:h + 1, :] - rb, MASKED))
            w_inter = jnp.exp(m_prev[:, h:h + 1] - rb)
            sc = scores[:, j * L:(j + 1) * L] * w_intra
            q_in = jnp.where(lane_half == j, qp.astype(F32) * w_inter, 0.0)
            v_aug = jnp.concatenate([v_ref[:, h * HEAD_W:(h + 1) * HEAD_W], ones], axis=1)
            c_aug = c_scr[h]
            lhs = jnp.concatenate([sc.astype(BF16), q_in.astype(BF16)], axis=1)
            rhs = jnp.concatenate([v_aug, c_aug.astype(BF16)], axis=0)
            nd = jnp.dot(lhs, rhs, preferred_element_type=F32)
            mb = jnp.broadcast_to(mstab[:, h:h + 1], (L, HEAD_W))
            hb = nd[:, :HEAD_W] / jnp.maximum(jnp.abs(nd[:, HEAD_W:]), jnp.exp(-mb))
            y_ref[:, h * HEAD_W:(h + 1) * HEAD_W] = (
                hb * og_ref[:, h * HEAD_W:(h + 1) * HEAD_W].astype(F32)).astype(y_ref.dtype)
            kw_t = jnp.where(key_half == j, k_t, 0.0) * ws_t[h:h + 1, :]
            c_scr[h] = decay[:, h:h + 1] * c_aug + jnp.dot(kw_t.astype(BF16), v_aug, preferred_element_type=F32)

    @pl.when(chunk == pl.num_programs(1) - 1)
    def _():
        cfin_ref[...] = c_scr[...]
        mfin_ref[...] = m_scr[...]


def _a_scan(q, k, v, og, gates, c0, m0, nb, seq):
    L = SCAN_CHUNK
    nc = seq // L
    row = lambda c: pl.BlockSpec((L, c), lambda b, i: (b * nc + i, 0))
    d_v = v.shape[1]
    return pl.pallas_call(
        _a_scan_kernel,
        grid=(nb, nc),
        in_specs=[row(q.shape[1]), row(k.shape[1]), row(d_v), row(d_v), row(2 * LANES),
                  _const_spec(c0.shape), _const_spec(m0.shape)],
        out_specs=[row(d_v), pl.BlockSpec(c0.shape, lambda b, i: (0, 0, 0)),
                   pl.BlockSpec(m0.shape, lambda b, i: (0, 0))],
        out_shape=[jax.ShapeDtypeStruct((nb * seq, d_v), BF16), jax.ShapeDtypeStruct(c0.shape, F32),
                   jax.ShapeDtypeStruct(m0.shape, F32)],
        scratch_shapes=[pltpu.VMEM(c0.shape, F32), pltpu.VMEM(m0.shape, F32)],
        compiler_params=_params(2),
        name="mlstm_scan",
    )(q, k, v, og, gates, c0, m0)


def _b_proj_kernel(h_ref, g_ref, w_ref, fb_ref, lbraw_ref, q_ref, k_ref, f_ref, v_ref, sg_ref, u_ref, *, sub):
    d_f = B_HEADS * HEAD_W
    raw = lbraw_ref[...]
    e = jnp.exp(raw - jnp.max(raw, axis=0, keepdims=True))
    sm = e / jnp.sum(e, axis=0, keepdims=True)
    lb_all = (sm[0:1, :] + sm[1:2, :]) - sm[0:1, :]

    def put_heads(ref, rows, c, x):
        ref[c // HEAD_W, rows, :] = x[:, :HEAD_W]
        ref[c // HEAD_W + 1, rows, :] = x[:, HEAD_W:]

    for rows in _sub_tiles(h_ref.shape[0], sub):
        u_ref[rows, :] = _rmsnorm(h_ref[rows, :], g_ref[...]).astype(BF16)
        for c in range(0, d_f, PROJ_TILE):
            cols = slice(c, c + PROJ_TILE)
            lb = lb_all[:, cols]
            fpre = jnp.dot(u_ref[rows, :], w_ref[:, d_f + c:d_f + c + PROJ_TILE],
                           preferred_element_type=F32) + fb_ref[:, cols]
            en = jnp.exp(-jnp.abs(fpre))
            big = 1.0 / (1.0 + en)
            small = en * big
            pos = fpre >= 0.0
            put_heads(f_ref, rows, c, lb + (1.0 - lb) * jnp.where(pos, big, small))
            put_heads(k_ref, rows, c, ((1.0 - lb) * jnp.where(pos, small, big)).astype(BF16))
            v = jnp.dot(u_ref[rows, :], w_ref[:, 2 * d_f + c:2 * d_f + c + PROJ_TILE], preferred_element_type=F32)
            put_heads(v_ref, rows, c, v.astype(BF16))
            zq = jnp.dot(u_ref[rows, :], w_ref[:, cols], preferred_element_type=F32)
            put_heads(q_ref, rows, c, (zq * _sigmoid(zq)).astype(BF16))
            g = jnp.dot(u_ref[rows, :], w_ref[:, 3 * d_f + c:3 * d_f + c + PROJ_TILE], preferred_element_type=F32)
            sg_ref[rows, cols] = _sigmoid(g).astype(BF16)


def _b_proj(h, g, w, fb, lbraw, seq):
    m, d = h.shape
    tm = min(seq, PROJ_ROWS)
    heads = pl.BlockSpec((B_HEADS, tm, HEAD_W), lambda i: (0, i, 0))
    hshape = lambda dt: jax.ShapeDtypeStruct((B_HEADS, m, HEAD_W), dt)
    return pl.pallas_call(
        functools.partial(_b_proj_kernel, sub=min(tm, PROJ_SUB_ROWS)),
        grid=(m // tm,),
        in_specs=[pl.BlockSpec((tm, d), lambda i: (i, 0)), _const_spec((1, d)), _const_spec(w.shape),
                  _const_spec(fb.shape), _const_spec(lbraw.shape)],
        out_specs=[heads, heads, heads, heads, pl.BlockSpec((tm, B_HEADS * HEAD_W), lambda i: (i, 0))],
        out_shape=[hshape(BF16), hshape(BF16), hshape(F32), hshape(BF16),
                   jax.ShapeDtypeStruct((m, B_HEADS * HEAD_W), BF16)],
        scratch_shapes=[pltpu.VMEM((tm, d), BF16)],
        compiler_params=_params(1),
        name="hgrn_proj",
    )(h, g, w, fb, lbraw)


def _lower_halves(x, b):
    return jnp.concatenate([x[i:i + b] for i in range(0, x.shape[0], 2 * b)], axis=0)


def _upper_halves(x, b):
    return jnp.concatenate([x[i + b:i + 2 * b] for i in range(0, x.shape[0], 2 * b)], axis=0)


def _interleave_halves(lo, up, b):
    pieces = []
    for i in range(0, lo.shape[0], b):
        pieces += [lo[i:i + b], up[i:i + b]]
    return jnp.concatenate(pieces, axis=0)


def _b_scan_kernel(q_ref, k_ref, f_ref, v_ref, s0_ref, o_ref, sfin_ref, s_scr, mask_scr):
    L = SCAN_CHUNK
    n_levels = int(math.log2(L))
    chunk = pl.program_id(1)

    @pl.when(chunk == 0)
    def _():
        s_scr[...] = s0_ref[...]

    @pl.when((chunk == 0) & (pl.program_id(0) == 0))
    def _():
        r = lax.broadcasted_iota(jnp.int32, (L, L), 0)
        s = lax.broadcasted_iota(jnp.int32, (L, L), 1)
        mask_scr[0] = jnp.where(r == s, 1.0, 0.0)
        for lev in range(n_levels - 1):
            mask_scr[lev + 1] = jnp.where((r >> (lev + 1)) == (s >> (lev + 1)), 1.0, 0.0)

    def head(h, carry):
        qb = q_ref[h]
        kb = k_ref[h]
        q = qb.astype(F32)
        k = kb.astype(F32)
        v = v_ref[h]
        row = lax.broadcasted_iota(jnp.int32, (L, HEAD_W), 0)
        a = lax.dot_general(qb, kb, NT, preferred_element_type=F32) * mask_scr[0]
        tot = f_ref[h]
        pre = tot
        post = jnp.ones_like(tot)
        for lev in range(n_levels):
            b = 1 << lev
            if b >= SUBLANES:
                tot_lo, tot_up = _lower_halves(tot, b), _upper_halves(tot, b)
                pre_lo, pre_up = _lower_halves(pre, b), _upper_halves(pre, b)
                post_lo, post_up = _lower_halves(post, b), _upper_halves(post, b)
                zeros = jnp.zeros_like(tot_lo)
                qt = _interleave_halves(zeros, _upper_halves(q, b) * pre_up, b)
                kt = _interleave_halves(_lower_halves(k, b) * post_lo, zeros, b)
                pre = _interleave_halves(pre_lo, pre_up * tot_lo, b)
                post = _interleave_halves(post_lo * tot_up, post_up, b)
                both = tot_lo * tot_up
                tot = _interleave_halves(both, both, b)
            else:
                upper = (row & b) != 0
                below = pltpu.roll(tot, b, 0)
                above = pltpu.roll(tot, L - b, 0)
                qt = jnp.where(upper, q * pre, 0.0)
                kt = jnp.where(upper, 0.0, k * post)
                pre = pre * jnp.where(upper, below, 1.0)
                post = post * jnp.where(upper, 1.0, above)
                tot = tot * jnp.where(upper, below, above)
            p = lax.dot_general(qt.astype(BF16), kt.astype(BF16), NT, preferred_element_type=F32)
            a = a + (p if lev == n_levels - 1 else p * mask_scr[lev + 1])
        st = s_scr[h]
        o = (jnp.dot(a.astype(BF16), v, preferred_element_type=F32)
             + lax.dot_general((q * pre).astype(BF16), st.astype(BF16), NT, preferred_element_type=F32))
        o_ref[h] = o.astype(o_ref.dtype)
        kd = (k * post).astype(BF16)
        s_scr[h] = st * tot[0:1, :] + lax.dot_general(v, kd, TN, preferred_element_type=F32)
        return carry

    lax.fori_loop(0, B_HEADS, head, 0, unroll=True)

    @pl.when(chunk == pl.num_programs(1) - 1)
    def _():
        sfin_ref[...] = s_scr[...]


def _b_scan(q, k, f, v, s0, nb, seq):
    L = SCAN_CHUNK
    nc = seq // L
    heads = pl.BlockSpec((B_HEADS, L, HEAD_W), lambda b, i: (0, b * nc + i, 0))
    n_masks = int(math.log2(L))
    return pl.pallas_call(
        _b_scan_kernel,
        grid=(nb, nc),
        in_specs=[heads, heads, heads, heads, _const_spec(s0.shape)],
        out_specs=[heads, pl.BlockSpec(s0.shape, lambda b, i: (0, 0, 0))],
        out_shape=[jax.ShapeDtypeStruct((B_HEADS, nb * seq, HEAD_W), BF16), jax.ShapeDtypeStruct(s0.shape, F32)],
        scratch_shapes=[pltpu.VMEM(s0.shape, F32), pltpu.VMEM((n_masks, L, L), F32)],
        compiler_params=_params(2),
        name="hgrn_scan",
    )(q, k, f, v, s0)


def _pad_rows(x, axis, n, value=0.0):
    pad = [(0, 0)] * x.ndim
    pad[axis] = (0, n - x.shape[axis])
    return jnp.pad(x, pad, constant_values=value)


def _trunk(h, p, state, nb, seq):
    gn = p["gains"]
    scan_seq = max(seq, SCAN_CHUNK)
    short = scan_seq != seq

    h = _ffn(h, gn[0][0], gn[0][1], p["ffn_in"][0][0], p["ffn_out"][0][0])
    q, k, v, og, gates, tail = _a_proj(h, gn[0][2], p["a_w"], p["a_wg"], p["a_cw"], p["a_cb"], p["a_gb"],
                                      state["tail"], seq)
    if short:
        q, k, v, og = (_pad_rows(t, 0, scan_seq) for t in (q, k, v, og))
        lane = jnp.arange(2 * LANES)[None, :]
        pad_gate = jnp.where(lane < LANES, MASKED, 0.0).astype(F32)
        gates = jnp.concatenate([gates, jnp.broadcast_to(pad_gate, (scan_seq - seq, 2 * LANES))], axis=0)
    y, c_fin, m_fin = _a_scan(q, k, v, og, gates, state["c"], state["m"], nb, scan_seq)
    if short:
        y = y[:seq]
    h = _ffn(h, gn[0][4], gn[0][5], p["ffn_in"][0][1], p["ffn_out"][0][1], mixer=("a", y, p["a_wo"], gn[0][3]))

    h = _ffn(h, gn[1][0], gn[1][1], p["ffn_in"][1][0], p["ffn_out"][1][0])
    q, k, f, v, sg = _b_proj(h, gn[1][2], p["b_w"], p["b_fb"], p["b_lbraw"], seq)
    if short:
        q, k, v = (_pad_rows(t, 1, scan_seq) for t in (q, k, v))
        f = _pad_rows(f, 1, scan_seq, 1.0)
    o, s_fin = _b_scan(q, k, f, v, state["s"], nb, scan_seq)
    if short:
        o = o[:, :seq]
    h = _ffn(h, gn[1][4], gn[1][5], p["ffn_in"][1][1], p["ffn_out"][1][1],
             mixer=("b", o, sg, p["b_gn"], p["b_wo"], gn[1][3]))
    return h, {"tail": tail, "c": c_fin, "m": m_fin, "s": s_fin}


def kernel(x, meta_tokens, norm_gains, ffn_w_in, ffn_w_out, a_w_in, a_conv_w, a_conv_b, a_gate_b, a_w_out,
           b_w_in, b_f_bias, b_lb_raw, b_g_norm, b_w_out):
    bsz, seq, d = x.shape
    depth = norm_gains.shape[0]
    assert depth == 2 and a_w_in.shape[0] == 1 and b_w_in.shape[0] == 1
    assert ffn_w_out.shape[2] % FF_TILE == 0
    d_qkvo = 2 * A_HEADS * A_DK + 2 * A_HEADS * HEAD_W
    ffn_in_bf16 = ffn_w_in.astype(BF16)
    ffn_out_bf16 = ffn_w_out.astype(BF16)

    def gate_lanes(t):
        z = jnp.zeros(t.shape[:-1] + (LANES - A_HEADS,), t.dtype)
        return jnp.concatenate([t[..., :A_HEADS], z, t[..., A_HEADS:], z], axis=-1)

    p = {
        "gains": [[norm_gains[i, j][None, :] for j in range(6)] for i in range(depth)],
        "ffn_in": [[ffn_in_bf16[i, j] for j in range(2)] for i in range(depth)],
        "ffn_out": [[ffn_out_bf16[i, j] for j in range(2)] for i in range(depth)],
        "a_w": a_w_in[0][:, :d_qkvo].astype(BF16),
        "a_wg": gate_lanes(a_w_in[0][:, d_qkvo:]).astype(BF16),
        "a_cw": a_conv_w[0], "a_cb": a_conv_b[0][None, :],
        "a_gb": gate_lanes(a_gate_b[0][None, :]),
        "a_wo": a_w_out[0].astype(BF16),
        "b_w": b_w_in[0].astype(BF16), "b_fb": b_f_bias[0][None, :], "b_lbraw": b_lb_raw,
        "b_gn": b_g_norm[0][None, :], "b_wo": b_w_out[0].astype(BF16),
    }
    state0 = {
        "tail": jnp.zeros((SUBLANES, 2 * A_HEADS * A_DK), F32),
        "c": jnp.zeros((A_HEADS, LANES, 2 * HEAD_W), F32),
        "m": jnp.zeros((1, LANES), F32),
        "s": jnp.zeros((B_HEADS, HEAD_W, HEAD_W), F32),
    }
    _, state = _trunk(meta_tokens.astype(x.dtype), p, state0, 1, N_META)
    out, _ = _trunk(x.reshape(bsz * seq, d), p, state, bsz, seq)
    return out.reshape(bsz, seq, d)
```

```python
import functools
import math

import jax
import jax.numpy as jnp
from jax import lax
from jax.experimental import pallas as pl
from jax.experimental.pallas import tpu as pltpu

LANES = 128
SUBLANES = 8
VMEM_LIMIT_BYTES = 56 * 1024 * 1024

N_META = 16
EPS = 1e-6
A_HEADS = 8
A_DK = 64
A_CONV = 4
B_HEADS = 8
HEAD_W = 128
SCAN_CHUNK = 128
FF_TILE = 256
FFN_ROWS = 512
FFN_SUB_ROWS = 256
SCAN_STAGES_PER_FFN_DOT = {"a": (1, 5), "b": (3, 16)}
FFN_OVERLAP_AT = 1
PROJ_ROWS = 512
PROJ_SUB_ROWS = 256
PROJ_TILE = 256
MASKED = -1e30

F32 = jnp.float32
BF16 = jnp.bfloat16
NT = (((1,), (1,)), ((), ()))
TN = (((0,), (0,)), ((), ()))

assert SCAN_CHUNK == LANES and 2 * A_DK == LANES and PROJ_TILE == 2 * HEAD_W


def _rmsnorm(x, g):
    return x * lax.rsqrt(jnp.mean(x * x, axis=-1, keepdims=True) + EPS) * g


def _sigmoid(x):
    return 1.0 / (1.0 + jnp.exp(-x))


def _log_sigmoid(x):
    return jnp.minimum(x, 0.0) - jnp.log1p(jnp.exp(-jnp.abs(x)))


def _params(n_grid_axes):
    return pltpu.CompilerParams(dimension_semantics=("arbitrary",) * n_grid_axes,
                                vmem_limit_bytes=VMEM_LIMIT_BYTES)


def _const_spec(shape):
    zeros = (0,) * len(shape)
    return pl.BlockSpec(shape, lambda *_: zeros, pipeline_mode=pl.Buffered(1))


def _sub_tiles(n_rows, sub):
    return [slice(r0, r0 + sub) for r0 in range(0, n_rows, sub)]


def _ffn_stages(mixer, h_ref, mix_refs, gpre_ref, gpost_ref, win_ref, wout_ref, o_ref, u_ref, act_ref):
    d_ff = wout_ref.shape[0]

    def ffn_input(rows):
        return h_ref[rows, :] if mixer is None else o_ref[rows, :]

    def prologue(rows):
        if mixer == "a":
            y_ref, wo_ref, gmix_ref = mix_refs
            mix = y_ref[rows, :]
        elif mixer == "b":
            o8_ref, sg_ref, gn_ref, wo_ref, gmix_ref = mix_refs
            o = jnp.concatenate([o8_ref[hd, rows, :].astype(F32) for hd in range(B_HEADS)], axis=-1)
            mix = _rmsnorm(o * sg_ref[rows, :].astype(F32), gn_ref[...]).astype(BF16)
        if mixer is not None:
            z = jnp.dot(mix, wo_ref[...], preferred_element_type=F32)
            o_ref[rows, :] = h_ref[rows, :] + _rmsnorm(z, gmix_ref[...])
        u_ref[rows, :] = _rmsnorm(ffn_input(rows), gpre_ref[...]).astype(BF16)

    def activation(rows, lo):
        gate = jnp.dot(u_ref[rows, :], win_ref[:, lo:lo + FF_TILE], preferred_element_type=F32)
        up = jnp.dot(u_ref[rows, :], win_ref[:, d_ff + lo:d_ff + lo + FF_TILE], preferred_element_type=F32)
        act_ref[rows, lo:lo + FF_TILE] = (gate * _sigmoid(gate) * up).astype(BF16)

    def down(rows):
        return jnp.dot(act_ref[rows, :], wout_ref[...], preferred_element_type=F32)

    def epilogue(rows, y):
        o_ref[rows, :] = ffn_input(rows) + 0.5 * _rmsnorm(y, gpost_ref[...])

    return prologue, activation, down, epilogue


def _ffn_kernel(*refs, mixer, sub):
    n_mix = {None: 0, "a": 3, "b": 5}[mixer]
    h_ref = refs[0]
    mix_refs = refs[1:1 + n_mix]
    gpre_ref, gpost_ref, win_ref, wout_ref, o_ref, u_ref, act_ref, y_scr = refs[1 + n_mix:]
    prologue, activation, down, epilogue = _ffn_stages(
        mixer, h_ref, mix_refs, gpre_ref, gpost_ref, win_ref, wout_ref, o_ref, u_ref, act_ref)
    tiles = _sub_tiles(h_ref.shape[0], sub)

    prologue(tiles[0])
    for t, rows in enumerate(tiles):
        for n, lo in enumerate(range(0, wout_ref.shape[0], FF_TILE)):
            activation(rows, lo)
            if n == FFN_OVERLAP_AT:
                if t + 1 < len(tiles):
                    prologue(tiles[t + 1])
                if t > 0:
                    epilogue(tiles[t - 1], y_scr[tiles[t - 1], :])
        y_scr[rows, :] = down(rows)
    epilogue(tiles[-1], y_scr[tiles[-1], :])


def _ffn(h, g_pre, g_post, w_in, w_out, mixer=None):
    m, d = h.shape
    d_ff = w_out.shape[0]
    tm = min(m, FFN_ROWS)
    row = lambda c: pl.BlockSpec((tm, c), lambda i: (i, 0))
    vec = _const_spec((1, d))
    if mixer is None:
        kind, name = None, "ffn"
        args, specs = [], []
    elif mixer[0] == "a":
        kind, name = "a", "mlstm_out_ffn"
        _, y, w_o, g_mix = mixer
        args = [y, w_o, g_mix]
        specs = [row(y.shape[1]), _const_spec(w_o.shape), vec]
    else:
        kind, name = "b", "hgrn_out_ffn"
        _, o8, sg, g_norm, w_o, g_mix = mixer
        args = [o8, sg, g_norm, w_o, g_mix]
        specs = [pl.BlockSpec((B_HEADS, tm, HEAD_W), lambda i: (0, i, 0)), row(sg.shape[1]),
                 _const_spec(g_norm.shape), _const_spec(w_o.shape), vec]
    return pl.pallas_call(
        functools.partial(_ffn_kernel, mixer=kind, sub=min(tm, FFN_SUB_ROWS)),
        grid=(m // tm,),
        in_specs=[row(d)] + specs + [vec, vec, _const_spec(w_in.shape), _const_spec(w_out.shape)],
        out_specs=row(d),
        out_shape=jax.ShapeDtypeStruct((m, d), F32),
        scratch_shapes=[pltpu.VMEM((tm, d), BF16), pltpu.VMEM((tm, d_ff), BF16), pltpu.VMEM((tm, d), F32)],
        compiler_params=_params(1),
        name=name,
    )(h, *args, g_pre, g_post, w_in, w_out)


def _a_proj_kernel(h_ref, g_ref, w_ref, wg_ref, cw_ref, cb_ref, gb_ref, tail0_ref,
                   q_ref, k_ref, v_ref, og_ref, gates_ref, tail_ref, z_scr, u_ref, *, tiles_per_seq, sub):
    d_qk = A_HEADS * A_DK
    d_v = A_HEADS * HEAD_W

    @pl.when(pl.program_id(0) % tiles_per_seq == 0)
    def _():
        z_scr[:SUBLANES, :] = tail0_ref[...]

    for rows in _sub_tiles(h_ref.shape[0], sub):
        u_ref[rows, :] = _rmsnorm(h_ref[rows, :], g_ref[...]).astype(BF16)
        for c in range(0, 2 * d_qk, PROJ_TILE):
            cols = slice(c, c + PROJ_TILE)
            z = jnp.dot(u_ref[rows, :], w_ref[:, cols], preferred_element_type=F32)
            z_scr[SUBLANES:, cols] = z
            acc = cb_ref[:, cols] + cw_ref[A_CONV - 1:A_CONV, cols] * z
            for j in range(1, A_CONV):
                acc = acc + cw_ref[A_CONV - 1 - j:A_CONV - j, cols] * z_scr[SUBLANES - j:SUBLANES - j + sub, cols]
            z_scr[:SUBLANES, cols] = z[sub - SUBLANES:, :]
            act = acc * _sigmoid(acc)
            if c < d_qk:
                q_ref[rows, cols] = (act * (A_DK ** -0.5)).astype(BF16)
            else:
                k_ref[rows, c - d_qk:c - d_qk + PROJ_TILE] = act.astype(BF16)
            v_ref[rows, cols] = jnp.dot(u_ref[rows, :], w_ref[:, 2 * d_qk + c:2 * d_qk + c + PROJ_TILE],
                                        preferred_element_type=F32).astype(BF16)
            og = jnp.dot(u_ref[rows, :], w_ref[:, 2 * d_qk + d_v + c:2 * d_qk + d_v + c + PROJ_TILE],
                         preferred_element_type=F32)
            og_ref[rows, cols] = _sigmoid(og).astype(BF16)
        g = jnp.dot(u_ref[rows, :], wg_ref[...], preferred_element_type=F32) + gb_ref[...]
        gates_ref[rows, :LANES] = g[:, :LANES]
        gates_ref[rows, LANES:] = _log_sigmoid(g[:, LANES:])
    tail_ref[...] = z_scr[:SUBLANES, :]


def _a_proj(h, g, w, wg, cw, cb, gb, tail0, seq):
    m, d = h.shape
    tm = min(seq, PROJ_ROWS)
    sub = min(tm, PROJ_SUB_ROWS)
    d_qk = A_HEADS * A_DK
    d_v = A_HEADS * HEAD_W
    assert d_v == 2 * d_qk
    row = lambda c: pl.BlockSpec((tm, c), lambda i: (i, 0))
    return pl.pallas_call(
        functools.partial(_a_proj_kernel, tiles_per_seq=seq // tm, sub=sub),
        grid=(m // tm,),
        in_specs=[row(d), _const_spec((1, d)), _const_spec(w.shape), _const_spec(wg.shape),
                  _const_spec(cw.shape), _const_spec(cb.shape), _const_spec(gb.shape), _const_spec(tail0.shape)],
        out_specs=[row(d_qk), row(d_qk), row(d_v), row(d_v), row(2 * LANES),
                   pl.BlockSpec((SUBLANES, 2 * d_qk), lambda i: (0, 0))],
        out_shape=[jax.ShapeDtypeStruct((m, d_qk), BF16), jax.ShapeDtypeStruct((m, d_qk), BF16),
                   jax.ShapeDtypeStruct((m, d_v), BF16), jax.ShapeDtypeStruct((m, d_v), BF16),
                   jax.ShapeDtypeStruct((m, 2 * LANES), F32), jax.ShapeDtypeStruct((SUBLANES, 2 * d_qk), F32)],
        scratch_shapes=[pltpu.VMEM((SUBLANES + sub, 2 * d_qk), F32), pltpu.VMEM((tm, d), BF16)],
        compiler_params=_params(1),
        name="mlstm_proj",
    )(h, g, w, wg, cw, cb, gb, tail0)


def _mlstm_chunk_stages(q_ref, k_ref, v_ref, og_ref, gates_ref, rows, c_scr, m_scr, c0_ref, m0_ref, reset, y_ref):
    L = SCAN_CHUNK
    r = lax.broadcasted_iota(jnp.int32, (L, L), 0)
    s = lax.broadcasted_iota(jnp.int32, (L, L), 1)
    causal = s <= r
    tri = jnp.where(causal, 1.0, 0.0).astype(F32)
    fcum = jnp.dot(tri, gates_ref[rows, LANES:], preferred_element_type=F32, precision=lax.Precision.HIGHEST)
    a = gates_ref[rows, :LANES] - fcum
    row = lax.broadcasted_iota(jnp.int32, (L, LANES), 0)
    run = a
    for lev in range(int(math.log2(L))):
        b = 1 << lev
        run = jnp.maximum(run, jnp.where(row >= b, pltpu.roll(run, b, 0), MASKED))
    m_prev = m_scr[...]
    if reset is not None:
        m_prev = jnp.where(reset, m0_ref[...], m_prev)
    rmax = jnp.maximum(run, m_prev)
    mstab = fcum + rmax
    r_last = rmax[L - 1:L, :]
    decay = jnp.exp(m_prev - r_last)
    a_t = a.T
    ws_t = jnp.exp(a - r_last).T
    m_scr[...] = mstab[L - 1:L, :]
    yield

    key_half = lax.broadcasted_iota(jnp.int32, (LANES, 1), 0) // A_DK
    lane_half = lax.broadcasted_iota(jnp.int32, (1, LANES), 1) // A_DK
    ones = jnp.ones((L, HEAD_W), BF16)
    eye = jnp.where(r == s, 1.0, 0.0).astype(BF16)

    for pair in range(A_HEADS // 2):
        qp = q_ref[rows, pair * LANES:(pair + 1) * LANES]
        kp = k_ref[rows, pair * LANES:(pair + 1) * LANES]
        zero = jnp.zeros_like(kp)
        k_blocks = jnp.concatenate([jnp.where(lane_half == 0, kp, zero),
                                    jnp.where(lane_half == 1, kp, zero)], axis=0)
        scores = lax.dot_general(qp, k_blocks, NT, preferred_element_type=F32)
        k_t = lax.dot_general(eye, kp, NT, preferred_element_type=F32)
        yield
        for j in range(2):
            h = 2 * pair + j
            cols = slice(h * HEAD_W, (h + 1) * HEAD_W)
            rb = jnp.broadcast_to(rmax[:, h:h + 1], (L, L))
            w_intra = jnp.exp(jnp.where(causal, a_t[h:h + 1, :] - rb, MASKED))
            w_inter = jnp.exp(m_prev[:, h:h + 1] - rb)
            sc = scores[:, j * L:(j + 1) * L] * w_intra
            q_in = jnp.where(lane_half == j, qp.astype(F32) * w_inter, 0.0)
            v_aug = jnp.concatenate([v_ref[rows, cols], ones], axis=1)
            c_aug = c_scr[h]
            if reset is not None:
                c_aug = jnp.where(reset, c0_ref[h], c_aug)
            lhs = jnp.concatenate([sc.astype(BF16), q_in.astype(BF16)], axis=1)
            rhs = jnp.concatenate([v_aug, c_aug.astype(BF16)], axis=0)
            nd = jnp.dot(lhs, rhs, preferred_element_type=F32)
            yield
            mb = jnp.broadcast_to(mstab[:, h:h + 1], (L, HEAD_W))
            hb = nd[:, :HEAD_W] / jnp.maximum(jnp.abs(nd[:, HEAD_W:]), jnp.exp(-mb))
            y_ref[rows, cols] = (hb * og_ref[rows, cols].astype(F32)).astype(y_ref.dtype)
            kw_t = jnp.where(key_half == j, k_t, 0.0) * ws_t[h:h + 1, :]
            c_scr[h] = decay[:, h:h + 1] * c_aug + jnp.dot(kw_t.astype(BF16), v_aug, preferred_element_type=F32)
            yield


def _a_scan_kernel(q_ref, k_ref, v_ref, og_ref, gates_ref, c0_ref, m0_ref,
                   y_ref, cfin_ref, mfin_ref, c_scr, m_scr):
    chunk = pl.program_id(1)

    @pl.when(chunk == 0)
    def _():
        c_scr[...] = c0_ref[...]
        m_scr[...] = m0_ref[...]

    for _ in _mlstm_chunk_stages(q_ref, k_ref, v_ref, og_ref, gates_ref, slice(0, SCAN_CHUNK),
                                 c_scr, m_scr, c0_ref, m0_ref, None, y_ref):
        pass

    @pl.when(chunk == pl.num_programs(1) - 1)
    def _():
        cfin_ref[...] = c_scr[...]
        mfin_ref[...] = m_scr[...]


def _a_scan(q, k, v, og, gates, c0, m0, nb, seq):
    L = SCAN_CHUNK
    nc = seq // L
    row = lambda c: pl.BlockSpec((L, c), lambda b, i: (b * nc + i, 0))
    d_v = v.shape[1]
    return pl.pallas_call(
        _a_scan_kernel,
        grid=(nb, nc),
        in_specs=[row(q.shape[1]), row(k.shape[1]), row(d_v), row(d_v), row(2 * LANES),
                  _const_spec(c0.shape), _const_spec(m0.shape)],
        out_specs=[row(d_v), pl.BlockSpec(c0.shape, lambda b, i: (0, 0, 0)),
                   pl.BlockSpec(m0.shape, lambda b, i: (0, 0))],
        out_shape=[jax.ShapeDtypeStruct((nb * seq, d_v), BF16), jax.ShapeDtypeStruct(c0.shape, F32),
                   jax.ShapeDtypeStruct(m0.shape, F32)],
        scratch_shapes=[pltpu.VMEM(c0.shape, F32), pltpu.VMEM(m0.shape, F32)],
        compiler_params=_params(2),
        name="mlstm_scan",
    )(q, k, v, og, gates, c0, m0)


def _b_proj_kernel(h_ref, g_ref, w_ref, fb_ref, lbraw_ref, q_ref, k_ref, f_ref, v_ref, sg_ref, u_ref, *, sub):
    d_f = B_HEADS * HEAD_W
    raw = lbraw_ref[...]
    e = jnp.exp(raw - jnp.max(raw, axis=0, keepdims=True))
    sm = e / jnp.sum(e, axis=0, keepdims=True)
    lb_all = (sm[0:1, :] + sm[1:2, :]) - sm[0:1, :]

    def put_heads(ref, rows, c, x):
        ref[c // HEAD_W, rows, :] = x[:, :HEAD_W]
        ref[c // HEAD_W + 1, rows, :] = x[:, HEAD_W:]

    for rows in _sub_tiles(h_ref.shape[0], sub):
        u_ref[rows, :] = _rmsnorm(h_ref[rows, :], g_ref[...]).astype(BF16)
        for c in range(0, d_f, PROJ_TILE):
            cols = slice(c, c + PROJ_TILE)
            lb = lb_all[:, cols]
            fpre = jnp.dot(u_ref[rows, :], w_ref[:, d_f + c:d_f + c + PROJ_TILE],
                           preferred_element_type=F32) + fb_ref[:, cols]
            en = jnp.exp(-jnp.abs(fpre))
            big = 1.0 / (1.0 + en)
            small = en * big
            pos = fpre >= 0.0
            put_heads(f_ref, rows, c, lb + (1.0 - lb) * jnp.where(pos, big, small))
            put_heads(k_ref, rows, c, ((1.0 - lb) * jnp.where(pos, small, big)).astype(BF16))
            v = jnp.dot(u_ref[rows, :], w_ref[:, 2 * d_f + c:2 * d_f + c + PROJ_TILE], preferred_element_type=F32)
            put_heads(v_ref, rows, c, v.astype(BF16))
            zq = jnp.dot(u_ref[rows, :], w_ref[:, cols], preferred_element_type=F32)
            put_heads(q_ref, rows, c, (zq * _sigmoid(zq)).astype(BF16))
            g = jnp.dot(u_ref[rows, :], w_ref[:, 3 * d_f + c:3 * d_f + c + PROJ_TILE], preferred_element_type=F32)
            sg_ref[rows, cols] = _sigmoid(g).astype(BF16)


def _b_proj(h, g, w, fb, lbraw, seq):
    m, d = h.shape
    tm = min(seq, PROJ_ROWS)
    heads = pl.BlockSpec((B_HEADS, tm, HEAD_W), lambda i: (0, i, 0))
    hshape = lambda dt: jax.ShapeDtypeStruct((B_HEADS, m, HEAD_W), dt)
    return pl.pallas_call(
        functools.partial(_b_proj_kernel, sub=min(tm, PROJ_SUB_ROWS)),
        grid=(m // tm,),
        in_specs=[pl.BlockSpec((tm, d), lambda i: (i, 0)), _const_spec((1, d)), _const_spec(w.shape),
                  _const_spec(fb.shape), _const_spec(lbraw.shape)],
        out_specs=[heads, heads, heads, heads, pl.BlockSpec((tm, B_HEADS * HEAD_W), lambda i: (i, 0))],
        out_shape=[hshape(BF16), hshape(BF16), hshape(F32), hshape(BF16),
                   jax.ShapeDtypeStruct((m, B_HEADS * HEAD_W), BF16)],
        scratch_shapes=[pltpu.VMEM((tm, d), BF16)],
        compiler_params=_params(1),
        name="hgrn_proj",
    )(h, g, w, fb, lbraw)


def _lower_halves(x, b):
    return jnp.concatenate([x[i:i + b] for i in range(0, x.shape[0], 2 * b)], axis=0)


def _upper_halves(x, b):
    return jnp.concatenate([x[i + b:i + 2 * b] for i in range(0, x.shape[0], 2 * b)], axis=0)


def _interleave_halves(lo, up, b):
    pieces = []
    for i in range(0, lo.shape[0], b):
        pieces += [lo[i:i + b], up[i:i + b]]
    return jnp.concatenate(pieces, axis=0)


def _hgrn_init_masks(mask_scr):
    L = SCAN_CHUNK
    r = lax.broadcasted_iota(jnp.int32, (L, L), 0)
    s = lax.broadcasted_iota(jnp.int32, (L, L), 1)
    mask_scr[0] = jnp.where(r == s, 1.0, 0.0)
    for lev in range(int(math.log2(L)) - 1):
        mask_scr[lev + 1] = jnp.where((r >> (lev + 1)) == (s >> (lev + 1)), 1.0, 0.0)


def _hgrn_head_stages(qb, kb, f, v, st, mask_scr, result):
    L = SCAN_CHUNK
    n_levels = int(math.log2(L))
    q = qb.astype(F32)
    k = kb.astype(F32)
    row = lax.broadcasted_iota(jnp.int32, (L, HEAD_W), 0)
    a = lax.dot_general(qb, kb, NT, preferred_element_type=F32) * mask_scr[0]
    tot = f
    pre = f
    post = jnp.ones_like(f)
    for lev in range(n_levels):
        b = 1 << lev
        if b >= SUBLANES:
            tot_lo, tot_up = _lower_halves(tot, b), _upper_halves(tot, b)
            pre_lo, pre_up = _lower_halves(pre, b), _upper_halves(pre, b)
            post_lo, post_up = _lower_halves(post, b), _upper_halves(post, b)
            zeros = jnp.zeros_like(tot_lo)
            qt = _interleave_halves(zeros, _upper_halves(q, b) * pre_up, b)
            kt = _interleave_halves(_lower_halves(k, b) * post_lo, zeros, b)
            pre = _interleave_halves(pre_lo, pre_up * tot_lo, b)
            post = _interleave_halves(post_lo * tot_up, post_up, b)
            both = tot_lo * tot_up
            tot = _interleave_halves(both, both, b)
        else:
            upper = (row & b) != 0
            below = pltpu.roll(tot, b, 0)
            above = pltpu.roll(tot, L - b, 0)
            qt = jnp.where(upper, q * pre, 0.0)
            kt = jnp.where(upper, 0.0, k * post)
            pre = pre * jnp.where(upper, below, 1.0)
            post = post * jnp.where(upper, 1.0, above)
            tot = tot * jnp.where(upper, below, above)
        p = lax.dot_general(qt.astype(BF16), kt.astype(BF16), NT, preferred_element_type=F32)
        a = a + (p if lev == n_levels - 1 else p * mask_scr[lev + 1])
        yield
    o = (jnp.dot(a.astype(BF16), v, preferred_element_type=F32)
         + lax.dot_general((q * pre).astype(BF16), st.astype(BF16), NT, preferred_element_type=F32))
    kd = (k * post).astype(BF16)
    result += [o, st * tot[0:1, :] + lax.dot_general(v, kd, TN, preferred_element_type=F32)]
    yield


def _hgrn_head(qb, kb, f, v, st, mask_scr):
    result = []
    for _ in _hgrn_head_stages(qb, kb, f, v, st, mask_scr, result):
        pass
    return result


def _b_scan_kernel(q_ref, k_ref, f_ref, v_ref, s0_ref, o_ref, sfin_ref, s_scr, mask_scr):
    chunk = pl.program_id(1)

    @pl.when(chunk == 0)
    def _():
        s_scr[...] = s0_ref[...]

    @pl.when((chunk == 0) & (pl.program_id(0) == 0))
    def _():
        _hgrn_init_masks(mask_scr)

    for hd in range(B_HEADS):
        o, s_scr[hd] = _hgrn_head(q_ref[hd], k_ref[hd], f_ref[hd], v_ref[hd], s_scr[hd], mask_scr)
        o_ref[hd] = o.astype(o_ref.dtype)

    @pl.when(chunk == pl.num_programs(1) - 1)
    def _():
        sfin_ref[...] = s_scr[...]


def _b_scan(q, k, f, v, s0, nb, seq):
    L = SCAN_CHUNK
    nc = seq // L
    heads = pl.BlockSpec((B_HEADS, L, HEAD_W), lambda b, i: (0, b * nc + i, 0))
    n_masks = int(math.log2(L))
    return pl.pallas_call(
        _b_scan_kernel,
        grid=(nb, nc),
        in_specs=[heads, heads, heads, heads, _const_spec(s0.shape)],
        out_specs=[heads, pl.BlockSpec(s0.shape, lambda b, i: (0, 0, 0))],
        out_shape=[jax.ShapeDtypeStruct((B_HEADS, nb * seq, HEAD_W), BF16), jax.ShapeDtypeStruct(s0.shape, F32)],
        scratch_shapes=[pltpu.VMEM(s0.shape, F32), pltpu.VMEM((n_masks, L, L), F32)],
        compiler_params=_params(2),
        name="hgrn_scan",
    )(q, k, f, v, s0)


def _interleave(main, filler=None):
    for n_filler_stages in main:
        for _ in range(n_filler_stages if filler is not None else 0):
            next(filler, None)
    if filler is not None:
        for _ in filler:
            pass


def _scan_ffn_kernel(*refs, kind, tiles_per_seq, n_tiles):
    if kind == "a":
        (q_ref, k_ref, v_ref, og_ref, gates_ref, c0_ref, m0_ref, h_ref, wo_ref, gmix_ref,
         gpre_ref, gpost_ref, win_ref, wout_ref, out_ref, c_scr, m_scr, o_scr, u_ref, act_ref) = refs
    else:
        (q_ref, k_ref, f_ref, v_ref, s0_ref, h_ref, sg_ref, gn_ref, wo_ref, gmix_ref,
         gpre_ref, gpost_ref, win_ref, wout_ref, out_ref, s_scr, mask_scr, o_scr, u_ref, act_ref) = refs
    step = pl.program_id(0)
    L = SCAN_CHUNK
    all_rows = slice(0, h_ref.shape[0])
    chunks = [slice(c * L, (c + 1) * L) for c in range(h_ref.shape[0] // L)]

    def scan_stream(tile):
        first = tile % tiles_per_seq == 0
        dst = o_scr.at[tile % 2]
        for c, rows in enumerate(chunks):
            if kind == "a":
                yield from _mlstm_chunk_stages(q_ref, k_ref, v_ref, og_ref, gates_ref, rows, c_scr, m_scr,
                                               c0_ref, m0_ref, first if c == 0 else None, dst)
                continue
            for hd in range(B_HEADS):
                st = s_scr[hd]
                if c == 0:
                    st = jnp.where(first, s0_ref[hd], st)
                result = []
                yield from _hgrn_head_stages(q_ref[hd, rows, :], k_ref[hd, rows, :], f_ref[hd, rows, :],
                                             v_ref[hd, rows, :], st, mask_scr, result)
                dst[hd, rows, :] = result[0].astype(BF16)
                s_scr[hd] = result[1]

    def ffn_stream(tile, quota_in, quota_down):
        src = o_scr.at[tile % 2]
        mix_refs = (src, wo_ref, gmix_ref) if kind == "a" else (src, sg_ref, gn_ref, wo_ref, gmix_ref)
        prologue, _, _, epilogue = _ffn_stages(
            kind, h_ref, mix_refs, gpre_ref, gpost_ref, win_ref, wout_ref, out_ref, u_ref, act_ref)
        d_ff, d = wout_ref.shape
        prologue(all_rows)
        yield 0
        for lo in range(0, d_ff, FF_TILE):
            gate = jnp.dot(u_ref[...], win_ref[:, lo:lo + FF_TILE], preferred_element_type=F32)
            yield quota_in
            up = jnp.dot(u_ref[...], win_ref[:, d_ff + lo:d_ff + lo + FF_TILE], preferred_element_type=F32)
            act_ref[:, lo:lo + FF_TILE] = (gate * _sigmoid(gate) * up).astype(BF16)
            yield quota_in
        y = []
        for c in range(0, d, PROJ_TILE):
            y.append(jnp.dot(act_ref[...], wout_ref[:, c:c + PROJ_TILE], preferred_element_type=F32))
            yield quota_down
        epilogue(all_rows, jnp.concatenate(y, axis=1))
        yield 0

    quota_in, quota_down = SCAN_STAGES_PER_FFN_DOT[kind]

    @pl.when(step == 0)
    def _():
        if kind == "a":
            c_scr[...] = c0_ref[...]
            m_scr[...] = m0_ref[...]
        else:
            _hgrn_init_masks(mask_scr)
            s_scr[...] = s0_ref[...]
        _interleave(scan_stream(0))

    @pl.when((step > 0) & (step < n_tiles))
    def _():
        _interleave(ffn_stream(step - 1, quota_in, quota_down), scan_stream(step))

    @pl.when(step == n_tiles)
    def _():
        _interleave(ffn_stream(n_tiles - 1, 0, 0))


def _scan_ffn(kind, scan_args, h, mix_args, g_pre, g_post, w_in, w_out, seq):
    m, d = h.shape
    d_ff = w_out.shape[0]
    tile = FFN_SUB_ROWS
    n_tiles = m // tile
    scan_tile = lambda i: jnp.minimum(i, n_tiles - 1)
    ffn_row = lambda c: pl.BlockSpec((tile, c), lambda i: (jnp.maximum(i - 1, 0), 0))
    vec = _const_spec((1, d))
    if kind == "a":
        q, k, v, og, gates, c0, m0 = scan_args
        w_o, g_mix = mix_args
        srow = lambda c: pl.BlockSpec((tile, c), lambda i: (scan_tile(i), 0))
        scan_specs = [srow(q.shape[1]), srow(k.shape[1]), srow(v.shape[1]), srow(og.shape[1]),
                      srow(gates.shape[1]), _const_spec(c0.shape), _const_spec(m0.shape)]
        mix_specs = [_const_spec(w_o.shape), vec]
        scan_scratch = [pltpu.VMEM(c0.shape, F32), pltpu.VMEM(m0.shape, F32),
                        pltpu.VMEM((2, tile, v.shape[1]), BF16)]
        name = "mlstm_scan_ffn"
    else:
        q, k, f, v, s0 = scan_args
        sg, g_norm, w_o, g_mix = mix_args
        heads = pl.BlockSpec((B_HEADS, tile, HEAD_W), lambda i: (0, scan_tile(i), 0))
        scan_specs = [heads, heads, heads, heads, _const_spec(s0.shape)]
        mix_specs = [ffn_row(sg.shape[1]), _const_spec(g_norm.shape), _const_spec(w_o.shape), vec]
        scan_scratch = [pltpu.VMEM(s0.shape, F32),
                        pltpu.VMEM((int(math.log2(SCAN_CHUNK)), SCAN_CHUNK, SCAN_CHUNK), F32),
                        pltpu.VMEM((2, B_HEADS, tile, HEAD_W), BF16)]
        name = "hgrn_scan_ffn"
    return pl.pallas_call(
        functools.partial(_scan_ffn_kernel, kind=kind, tiles_per_seq=seq // tile, n_tiles=n_tiles),
        grid=(n_tiles + 1,),
        in_specs=scan_specs + [ffn_row(d)] + mix_specs + [vec, vec, _const_spec(w_in.shape), _const_spec(w_out.shape)],
        out_specs=ffn_row(d),
        out_shape=jax.ShapeDtypeStruct((m, d), F32),
        scratch_shapes=scan_scratch + [pltpu.VMEM((tile, d), BF16), pltpu.VMEM((tile, d_ff), BF16)],
        compiler_params=_params(1),
        name=name,
    )(*scan_args, h, *mix_args, g_pre, g_post, w_in, w_out)


def _pad_rows(x, axis, n, value=0.0):
    pad = [(0, 0)] * x.ndim
    pad[axis] = (0, n - x.shape[axis])
    return jnp.pad(x, pad, constant_values=value)


def _trunk(h, p, state, nb, seq):
    gn = p["gains"]
    scan_seq = max(seq, SCAN_CHUNK)
    short = scan_seq != seq

    h = _ffn(h, gn[0][0], gn[0][1], p["ffn_in"][0][0], p["ffn_out"][0][0])
    q, k, v, og, gates, tail = _a_proj(h, gn[0][2], p["a_w"], p["a_wg"], p["a_cw"], p["a_cb"], p["a_gb"],
                                      state["tail"], seq)
    if short:
        q, k, v, og = (_pad_rows(t, 0, scan_seq) for t in (q, k, v, og))
        lane = jnp.arange(2 * LANES)[None, :]
        pad_gate = jnp.where(lane < LANES, MASKED, 0.0).astype(F32)
        gates = jnp.concatenate([gates, jnp.broadcast_to(pad_gate, (scan_seq - seq, 2 * LANES))], axis=0)
        y, c_fin, m_fin = _a_scan(q, k, v, og, gates, state["c"], state["m"], nb, scan_seq)
        h = _ffn(h, gn[0][4], gn[0][5], p["ffn_in"][0][1], p["ffn_out"][0][1],
                 mixer=("a", y[:seq], p["a_wo"], gn[0][3]))
    else:
        c_fin = m_fin = None
        h = _scan_ffn("a", (q, k, v, og, gates, state["c"], state["m"]), h, (p["a_wo"], gn[0][3]),
                      gn[0][4], gn[0][5], p["ffn_in"][0][1], p["ffn_out"][0][1], seq)

    h = _ffn(h, gn[1][0], gn[1][1], p["ffn_in"][1][0], p["ffn_out"][1][0])
    q, k, f, v, sg = _b_proj(h, gn[1][2], p["b_w"], p["b_fb"], p["b_lbraw"], seq)
    if short:
        q, k, v = (_pad_rows(t, 1, scan_seq) for t in (q, k, v))
        f = _pad_rows(f, 1, scan_seq, 1.0)
        o, s_fin = _b_scan(q, k, f, v, state["s"], nb, scan_seq)
        h = _ffn(h, gn[1][4], gn[1][5], p["ffn_in"][1][1], p["ffn_out"][1][1],
                 mixer=("b", o[:, :seq], sg, p["b_gn"], p["b_wo"], gn[1][3]))
    else:
        s_fin = None
        h = _scan_ffn("b", (q, k, f, v, state["s"]), h, (sg, p["b_gn"], p["b_wo"], gn[1][3]), gn[1][4], gn[1][5],
                      p["ffn_in"][1][1], p["ffn_out"][1][1], seq)
    return h, {"tail": tail, "c": c_fin, "m": m_fin, "s": s_fin}


def kernel(x, meta_tokens, norm_gains, ffn_w_in, ffn_w_out, a_w_in, a_conv_w, a_conv_b, a_gate_b, a_w_out,
           b_w_in, b_f_bias, b_lb_raw, b_g_norm, b_w_out):
    bsz, seq, d = x.shape
    depth = norm_gains.shape[0]
    assert depth == 2 and a_w_in.shape[0] == 1 and b_w_in.shape[0] == 1
    assert ffn_w_out.shape[2] % FF_TILE == 0
    d_qkvo = 2 * A_HEADS * A_DK + 2 * A_HEADS * HEAD_W
    ffn_in_bf16 = ffn_w_in.astype(BF16)
    ffn_out_bf16 = ffn_w_out.astype(BF16)

    def gate_lanes(t):
        z = jnp.zeros(t.shape[:-1] + (LANES - A_HEADS,), t.dtype)
        return jnp.concatenate([t[..., :A_HEADS], z, t[..., A_HEADS:], z], axis=-1)

    p = {
        "gains": [[norm_gains[i, j][None, :] for j in range(6)] for i in range(depth)],
        "ffn_in": [[ffn_in_bf16[i, j] for j in range(2)] for i in range(depth)],
        "ffn_out": [[ffn_out_bf16[i, j] for j in range(2)] for i in range(depth)],
        "a_w": a_w_in[0][:, :d_qkvo].astype(BF16),
        "a_wg": gate_lanes(a_w_in[0][:, d_qkvo:]).astype(BF16),
        "a_cw": a_conv_w[0], "a_cb": a_conv_b[0][None, :],
        "a_gb": gate_lanes(a_gate_b[0][None, :]),
        "a_wo": a_w_out[0].astype(BF16),
        "b_w": b_w_in[0].astype(BF16), "b_fb": b_f_bias[0][None, :], "b_lbraw": b_lb_raw,
        "b_gn": b_g_norm[0][None, :], "b_wo": b_w_out[0].astype(BF16),
    }
    state0 = {
        "tail": jnp.zeros((SUBLANES, 2 * A_HEADS * A_DK), F32),
        "c": jnp.zeros((A_HEADS, LANES, 2 * HEAD_W), F32),
        "m": jnp.zeros((1, LANES), F32),
        "s": jnp.zeros((B_HEADS, HEAD_W, HEAD_W), F32),
    }
    _, state = _trunk(meta_tokens.astype(x.dtype), p, state0, 1, N_META)
    out, _ = _trunk(x.reshape(bsz * seq, d), p, state, bsz, seq)
    return out.reshape(bsz, seq, d)
```

```python
import functools
import math

import jax
import jax.numpy as jnp
from jax import lax
from jax.experimental import pallas as pl
from jax.experimental.pallas import tpu as pltpu

LANES = 128
SUBLANES = 8
VMEM_LIMIT_BYTES = 56 * 1024 * 1024

N_META = 16
EPS = 1e-6
A_HEADS = 8
A_DK = 64
A_CONV = 4
B_HEADS = 8
HEAD_W = 128
SCAN_CHUNK = 128
FF_TILE = 256
FFN_ROWS = 512
FFN_SUB_ROWS = 256
FUSED_ROWS = 512
SCAN_STAGES_PER_FFN_DOT = {"a": (1, 5), "b": (3, 16)}
FFN_OVERLAP_AT = 1
PROJ_ROWS = 1024
PROJ_SUB_ROWS = 256
PROJ_TILE = 256
MASKED = -1e30

F32 = jnp.float32
BF16 = jnp.bfloat16
NT = (((1,), (1,)), ((), ()))
TN = (((0,), (0,)), ((), ()))

assert SCAN_CHUNK == LANES and 2 * A_DK == LANES and PROJ_TILE == 2 * HEAD_W


def _rmsnorm(x, g):
    return x * lax.rsqrt(jnp.mean(x * x, axis=-1, keepdims=True) + EPS) * g


def _sigmoid(x):
    return 1.0 / (1.0 + jnp.exp(-x))


def _log_sigmoid(x):
    return jnp.minimum(x, 0.0) - jnp.log1p(jnp.exp(-jnp.abs(x)))


def _params(n_grid_axes):
    return pltpu.CompilerParams(dimension_semantics=("arbitrary",) * n_grid_axes,
                                vmem_limit_bytes=VMEM_LIMIT_BYTES)


def _const_spec(shape):
    zeros = (0,) * len(shape)
    return pl.BlockSpec(shape, lambda *_: zeros, pipeline_mode=pl.Buffered(1))


class _Stacked:
    def __init__(self, array, *index):
        self.array, self.index = array, index
        self.shape = array.shape[len(index):]

    def spec(self):
        at = self.index + (0,) * len(self.shape)
        return pl.BlockSpec((None,) * len(self.index) + self.shape, lambda *_: at, pipeline_mode=pl.Buffered(1))


def _sub_tiles(n_rows, sub):
    return [slice(r0, r0 + sub) for r0 in range(0, n_rows, sub)]


def _ffn_stages(mixer, h_ref, mix_refs, gpre_ref, gpost_ref, win_ref, wout_ref, o_ref, u_ref, act_ref):
    d_ff = wout_ref.shape[0]

    def ffn_input(rows):
        return h_ref[rows, :] if mixer is None else o_ref[rows, :]

    def prologue(rows):
        if mixer == "a":
            y_ref, wo_ref, gmix_ref = mix_refs
            mix = y_ref[rows, :]
        elif mixer == "b":
            o8_ref, sg_ref, gn_ref, wo_ref, gmix_ref = mix_refs
            o = jnp.concatenate([o8_ref[hd, rows, :].astype(F32) for hd in range(B_HEADS)], axis=-1)
            mix = _rmsnorm(o * sg_ref[rows, :].astype(F32), gn_ref[...]).astype(BF16)
        if mixer is not None:
            z = jnp.dot(mix, wo_ref[...], preferred_element_type=F32)
            o_ref[rows, :] = h_ref[rows, :] + _rmsnorm(z, gmix_ref[...])
        u_ref[rows, :] = _rmsnorm(ffn_input(rows), gpre_ref[...]).astype(BF16)

    def activation(rows, lo):
        gate = jnp.dot(u_ref[rows, :], win_ref[:, lo:lo + FF_TILE], preferred_element_type=F32)
        up = jnp.dot(u_ref[rows, :], win_ref[:, d_ff + lo:d_ff + lo + FF_TILE], preferred_element_type=F32)
        act_ref[rows, lo:lo + FF_TILE] = (gate * _sigmoid(gate) * up).astype(BF16)

    def down(rows):
        return jnp.dot(act_ref[rows, :], wout_ref[...], preferred_element_type=F32)

    def epilogue(rows, y):
        o_ref[rows, :] = ffn_input(rows) + 0.5 * _rmsnorm(y, gpost_ref[...])

    return prologue, activation, down, epilogue


def _ffn_kernel(*refs, mixer, sub):
    n_mix = {None: 0, "a": 3, "b": 5}[mixer]
    h_ref = refs[0]
    mix_refs = refs[1:1 + n_mix]
    gpre_ref, gpost_ref, win_ref, wout_ref, o_ref, u_ref, act_ref, y_scr = refs[1 + n_mix:]
    prologue, activation, down, epilogue = _ffn_stages(
        mixer, h_ref, mix_refs, gpre_ref, gpost_ref, win_ref, wout_ref, o_ref, u_ref, act_ref)
    tiles = _sub_tiles(h_ref.shape[0], sub)

    prologue(tiles[0])
    for t, rows in enumerate(tiles):
        for n, lo in enumerate(range(0, wout_ref.shape[0], FF_TILE)):
            activation(rows, lo)
            if n == FFN_OVERLAP_AT:
                if t + 1 < len(tiles):
                    prologue(tiles[t + 1])
                if t > 0:
                    epilogue(tiles[t - 1], y_scr[tiles[t - 1], :])
        y_scr[rows, :] = down(rows)
    epilogue(tiles[-1], y_scr[tiles[-1], :])


def _ffn(h, g_pre, g_post, w_in, w_out, mixer=None):
    m, d = h.shape
    d_ff = w_out.shape[0]
    tm = min(m, FFN_ROWS)
    row = lambda c: pl.BlockSpec((tm, c), lambda i: (i, 0))
    vec = _const_spec((1, d))
    if mixer is None:
        kind, name = None, "ffn"
        args, specs = [], []
    elif mixer[0] == "a":
        kind, name = "a", "mlstm_out_ffn"
        _, y, w_o, g_mix = mixer
        args = [y, w_o, g_mix]
        specs = [row(y.shape[1]), _const_spec(w_o.shape), vec]
    else:
        kind, name = "b", "hgrn_out_ffn"
        _, o8, sg, g_norm, w_o, g_mix = mixer
        args = [o8, sg, g_norm, w_o, g_mix]
        specs = [pl.BlockSpec((B_HEADS, tm, HEAD_W), lambda i: (0, i, 0)), row(sg.shape[1]),
                 _const_spec(g_norm.shape), _const_spec(w_o.shape), vec]
    return pl.pallas_call(
        functools.partial(_ffn_kernel, mixer=kind, sub=min(tm, FFN_SUB_ROWS)),
        grid=(m // tm,),
        in_specs=[row(d)] + specs + [vec, vec, w_in.spec(), w_out.spec()],
        out_specs=row(d),
        out_shape=jax.ShapeDtypeStruct((m, d), F32),
        scratch_shapes=[pltpu.VMEM((tm, d), BF16), pltpu.VMEM((tm, d_ff), BF16), pltpu.VMEM((tm, d), F32)],
        compiler_params=_params(1),
        name=name,
    )(h, *args, g_pre, g_post, w_in.array, w_out.array)


def _a_proj_kernel(h_ref, g_ref, w_ref, wg_ref, cw_ref, cb_ref, gb_ref, tail0_ref,
                   q_ref, k_ref, v_ref, og_ref, gates_ref, tail_ref, z_scr, u_ref, *, tiles_per_seq, sub):
    d_qk = A_HEADS * A_DK
    d_v = A_HEADS * HEAD_W

    @pl.when(pl.program_id(0) % tiles_per_seq == 0)
    def _():
        z_scr[:SUBLANES, :] = tail0_ref[...]

    for rows in _sub_tiles(h_ref.shape[0], sub):
        u_ref[rows, :] = _rmsnorm(h_ref[rows, :], g_ref[...]).astype(BF16)
        for c in range(0, 2 * d_qk, PROJ_TILE):
            cols = slice(c, c + PROJ_TILE)
            z = jnp.dot(u_ref[rows, :], w_ref[:, cols], preferred_element_type=F32)
            z_scr[SUBLANES:, cols] = z
            acc = cb_ref[:, cols] + cw_ref[A_CONV - 1:A_CONV, cols] * z
            for j in range(1, A_CONV):
                acc = acc + cw_ref[A_CONV - 1 - j:A_CONV - j, cols] * z_scr[SUBLANES - j:SUBLANES - j + sub, cols]
            z_scr[:SUBLANES, cols] = z[sub - SUBLANES:, :]
            act = acc * _sigmoid(acc)
            if c < d_qk:
                q_ref[rows, cols] = (act * (A_DK ** -0.5)).astype(BF16)
            else:
                k_ref[rows, c - d_qk:c - d_qk + PROJ_TILE] = act.astype(BF16)
            v_ref[rows, cols] = jnp.dot(u_ref[rows, :], w_ref[:, 2 * d_qk + c:2 * d_qk + c + PROJ_TILE],
                                        preferred_element_type=F32).astype(BF16)
            og = jnp.dot(u_ref[rows, :], w_ref[:, 2 * d_qk + d_v + c:2 * d_qk + d_v + c + PROJ_TILE],
                         preferred_element_type=F32)
            og_ref[rows, cols] = _sigmoid(og).astype(BF16)
        g = jnp.dot(u_ref[rows, :], wg_ref[...], preferred_element_type=F32) + gb_ref[...]
        gates_ref[rows, :LANES] = g[:, :LANES]
        gates_ref[rows, LANES:] = _log_sigmoid(g[:, LANES:])
    tail_ref[...] = z_scr[:SUBLANES, :]


def _a_proj(h, g, w, wg, cw, cb, gb, tail0, seq):
    m, d = h.shape
    tm = min(seq, PROJ_ROWS)
    sub = min(tm, PROJ_SUB_ROWS)
    d_qk = A_HEADS * A_DK
    d_v = A_HEADS * HEAD_W
    assert d_v == 2 * d_qk
    row = lambda c: pl.BlockSpec((tm, c), lambda i: (i, 0))
    return pl.pallas_call(
        functools.partial(_a_proj_kernel, tiles_per_seq=seq // tm, sub=sub),
        grid=(m // tm,),
        in_specs=[row(d), _const_spec((1, d)), _const_spec(w.shape), _const_spec(wg.shape),
                  _const_spec(cw.shape), _const_spec(cb.shape), _const_spec(gb.shape), _const_spec(tail0.shape)],
        out_specs=[row(d_qk), row(d_qk), row(d_v), row(d_v), row(2 * LANES),
                   pl.BlockSpec((SUBLANES, 2 * d_qk), lambda i: (0, 0))],
        out_shape=[jax.ShapeDtypeStruct((m, d_qk), BF16), jax.ShapeDtypeStruct((m, d_qk), BF16),
                   jax.ShapeDtypeStruct((m, d_v), BF16), jax.ShapeDtypeStruct((m, d_v), BF16),
                   jax.ShapeDtypeStruct((m, 2 * LANES), F32), jax.ShapeDtypeStruct((SUBLANES, 2 * d_qk), F32)],
        scratch_shapes=[pltpu.VMEM((SUBLANES + sub, 2 * d_qk), F32), pltpu.VMEM((tm, d), BF16)],
        compiler_params=_params(1),
        name="mlstm_proj",
    )(h, g, w, wg, cw, cb, gb, tail0)


def _mlstm_chunk_stages(q_ref, k_ref, v_ref, og_ref, gates_ref, rows, c_scr, m_scr, c0_ref, m0_ref, reset, y_ref):
    L = SCAN_CHUNK
    r = lax.broadcasted_iota(jnp.int32, (L, L), 0)
    s = lax.broadcasted_iota(jnp.int32, (L, L), 1)
    causal = s <= r
    tri = jnp.where(causal, 1.0, 0.0).astype(F32)
    fcum = jnp.dot(tri, gates_ref[rows, LANES:], preferred_element_type=F32, precision=lax.Precision.HIGHEST)
    a = gates_ref[rows, :LANES] - fcum
    row = lax.broadcasted_iota(jnp.int32, (L, LANES), 0)
    run = a
    for lev in range(int(math.log2(L))):
        b = 1 << lev
        run = jnp.maximum(run, jnp.where(row >= b, pltpu.roll(run, b, 0), MASKED))
    m_prev = m_scr[...]
    if reset is not None:
        m_prev = jnp.where(reset, m0_ref[...], m_prev)
    rmax = jnp.maximum(run, m_prev)
    mstab = fcum + rmax
    r_last = rmax[L - 1:L, :]
    decay = jnp.exp(m_prev - r_last)
    a_t = a.T
    ws_t = jnp.exp(a - r_last).T
    m_scr[...] = mstab[L - 1:L, :]
    yield

    key_half = lax.broadcasted_iota(jnp.int32, (LANES, 1), 0) // A_DK
    lane_half = lax.broadcasted_iota(jnp.int32, (1, LANES), 1) // A_DK
    ones = jnp.ones((L, HEAD_W), BF16)
    eye = jnp.where(r == s, 1.0, 0.0).astype(BF16)

    for pair in range(A_HEADS // 2):
        qp = q_ref[rows, pair * LANES:(pair + 1) * LANES]
        kp = k_ref[rows, pair * LANES:(pair + 1) * LANES]
        zero = jnp.zeros_like(kp)
        k_blocks = jnp.concatenate([jnp.where(lane_half == 0, kp, zero),
                                    jnp.where(lane_half == 1, kp, zero)], axis=0)
        scores = lax.dot_general(qp, k_blocks, NT, preferred_element_type=F32)
        k_t = lax.dot_general(eye, kp, NT, preferred_element_type=F32)
        yield
        for j in range(2):
            h = 2 * pair + j
            cols = slice(h * HEAD_W, (h + 1) * HEAD_W)
            rb = jnp.broadcast_to(rmax[:, h:h + 1], (L, L))
            w_intra = jnp.exp(jnp.where(causal, a_t[h:h + 1, :] - rb, MASKED))
            w_inter = jnp.exp(m_prev[:, h:h + 1] - rb)
            sc = scores[:, j * L:(j + 1) * L] * w_intra
            q_in = jnp.where(lane_half == j, qp.astype(F32) * w_inter, 0.0)
            v_aug = jnp.concatenate([v_ref[rows, cols], ones], axis=1)
            c_aug = c_scr[h]
            if reset is not None:
                c_aug = jnp.where(reset, c0_ref[h], c_aug)
            lhs = jnp.concatenate([sc.astype(BF16), q_in.astype(BF16)], axis=1)
            rhs = jnp.concatenate([v_aug, c_aug.astype(BF16)], axis=0)
            nd = jnp.dot(lhs, rhs, preferred_element_type=F32)
            yield
            mb = jnp.broadcast_to(mstab[:, h:h + 1], (L, HEAD_W))
            hb = nd[:, :HEAD_W] / jnp.maximum(jnp.abs(nd[:, HEAD_W:]), jnp.exp(-mb))
            y_ref[rows, cols] = (hb * og_ref[rows, cols].astype(F32)).astype(y_ref.dtype)
            kw_t = jnp.where(key_half == j, k_t, 0.0) * ws_t[h:h + 1, :]
            c_scr[h] = decay[:, h:h + 1] * c_aug + jnp.dot(kw_t.astype(BF16), v_aug, preferred_element_type=F32)
            yield


def _a_scan_kernel(q_ref, k_ref, v_ref, og_ref, gates_ref, c0_ref, m0_ref,
                   y_ref, cfin_ref, mfin_ref, c_scr, m_scr):
    chunk = pl.program_id(1)

    @pl.when(chunk == 0)
    def _():
        c_scr[...] = c0_ref[...]
        m_scr[...] = m0_ref[...]

    for _ in _mlstm_chunk_stages(q_ref, k_ref, v_ref, og_ref, gates_ref, slice(0, SCAN_CHUNK),
                                 c_scr, m_scr, c0_ref, m0_ref, None, y_ref):
        pass

    @pl.when(chunk == pl.num_programs(1) - 1)
    def _():
        cfin_ref[...] = c_scr[...]
        mfin_ref[...] = m_scr[...]


def _a_scan(q, k, v, og, gates, c0, m0, nb, seq):
    L = SCAN_CHUNK
    nc = seq // L
    row = lambda c: pl.BlockSpec((L, c), lambda b, i: (b * nc + i, 0))
    d_v = v.shape[1]
    return pl.pallas_call(
        _a_scan_kernel,
        grid=(nb, nc),
        in_specs=[row(q.shape[1]), row(k.shape[1]), row(d_v), row(d_v), row(2 * LANES),
                  _const_spec(c0.shape), _const_spec(m0.shape)],
        out_specs=[row(d_v), pl.BlockSpec(c0.shape, lambda b, i: (0, 0, 0)),
                   pl.BlockSpec(m0.shape, lambda b, i: (0, 0))],
        out_shape=[jax.ShapeDtypeStruct((nb * seq, d_v), BF16), jax.ShapeDtypeStruct(c0.shape, F32),
                   jax.ShapeDtypeStruct(m0.shape, F32)],
        scratch_shapes=[pltpu.VMEM(c0.shape, F32), pltpu.VMEM(m0.shape, F32)],
        compiler_params=_params(2),
        name="mlstm_scan",
    )(q, k, v, og, gates, c0, m0)


def _b_proj_kernel(h_ref, g_ref, w_ref, fb_ref, lbraw_ref, q_ref, k_ref, f_ref, v_ref, sg_ref, u_ref, *, sub):
    d_f = B_HEADS * HEAD_W
    raw = lbraw_ref[...]
    e = jnp.exp(raw - jnp.max(raw, axis=0, keepdims=True))
    sm = e / jnp.sum(e, axis=0, keepdims=True)
    lb_all = (sm[0:1, :] + sm[1:2, :]) - sm[0:1, :]

    def put_heads(ref, rows, c, x):
        ref[c // HEAD_W, rows, :] = x[:, :HEAD_W]
        ref[c // HEAD_W + 1, rows, :] = x[:, HEAD_W:]

    for rows in _sub_tiles(h_ref.shape[0], sub):
        u_ref[rows, :] = _rmsnorm(h_ref[rows, :], g_ref[...]).astype(BF16)
        for c in range(0, d_f, PROJ_TILE):
            cols = slice(c, c + PROJ_TILE)
            lb = lb_all[:, cols]
            fpre = jnp.dot(u_ref[rows, :], w_ref[:, d_f + c:d_f + c + PROJ_TILE],
                           preferred_element_type=F32) + fb_ref[:, cols]
            en = jnp.exp(-jnp.abs(fpre))
            big = 1.0 / (1.0 + en)
            small = en * big
            pos = fpre >= 0.0
            put_heads(f_ref, rows, c, lb + (1.0 - lb) * jnp.where(pos, big, small))
            put_heads(k_ref, rows, c, ((1.0 - lb) * jnp.where(pos, small, big)).astype(BF16))
            v = jnp.dot(u_ref[rows, :], w_ref[:, 2 * d_f + c:2 * d_f + c + PROJ_TILE], preferred_element_type=F32)
            put_heads(v_ref, rows, c, v.astype(BF16))
            zq = jnp.dot(u_ref[rows, :], w_ref[:, cols], preferred_element_type=F32)
            put_heads(q_ref, rows, c, (zq * _sigmoid(zq)).astype(BF16))
            g = jnp.dot(u_ref[rows, :], w_ref[:, 3 * d_f + c:3 * d_f + c + PROJ_TILE], preferred_element_type=F32)
            sg_ref[rows, cols] = _sigmoid(g).astype(BF16)


def _b_proj(h, g, w, fb, lbraw, seq):
    m, d = h.shape
    tm = min(seq, PROJ_ROWS)
    heads = pl.BlockSpec((B_HEADS, tm, HEAD_W), lambda i: (0, i, 0))
    hshape = lambda dt: jax.ShapeDtypeStruct((B_HEADS, m, HEAD_W), dt)
    return pl.pallas_call(
        functools.partial(_b_proj_kernel, sub=min(tm, PROJ_SUB_ROWS)),
        grid=(m // tm,),
        in_specs=[pl.BlockSpec((tm, d), lambda i: (i, 0)), _const_spec((1, d)), _const_spec(w.shape),
                  _const_spec(fb.shape), _const_spec(lbraw.shape)],
        out_specs=[heads, heads, heads, heads, pl.BlockSpec((tm, B_HEADS * HEAD_W), lambda i: (i, 0))],
        out_shape=[hshape(BF16), hshape(BF16), hshape(F32), hshape(BF16),
                   jax.ShapeDtypeStruct((m, B_HEADS * HEAD_W), BF16)],
        scratch_shapes=[pltpu.VMEM((tm, d), BF16)],
        compiler_params=_params(1),
        name="hgrn_proj",
    )(h, g, w, fb, lbraw)


def _lower_halves(x, b):
    return jnp.concatenate([x[i:i + b] for i in range(0, x.shape[0], 2 * b)], axis=0)


def _upper_halves(x, b):
    return jnp.concatenate([x[i + b:i + 2 * b] for i in range(0, x.shape[0], 2 * b)], axis=0)


def _interleave_halves(lo, up, b):
    pieces = []
    for i in range(0, lo.shape[0], b):
        pieces += [lo[i:i + b], up[i:i + b]]
    return jnp.concatenate(pieces, axis=0)


def _hgrn_init_masks(mask_scr):
    L = SCAN_CHUNK
    r = lax.broadcasted_iota(jnp.int32, (L, L), 0)
    s = lax.broadcasted_iota(jnp.int32, (L, L), 1)
    mask_scr[0] = jnp.where(r == s, 1.0, 0.0)
    for lev in range(int(math.log2(L)) - 1):
        mask_scr[lev + 1] = jnp.where((r >> (lev + 1)) == (s >> (lev + 1)), 1.0, 0.0)


def _hgrn_head_stages(qb, kb, f, v, st, mask_scr, result):
    L = SCAN_CHUNK
    n_levels = int(math.log2(L))
    q = qb.astype(F32)
    k = kb.astype(F32)
    row = lax.broadcasted_iota(jnp.int32, (L, HEAD_W), 0)
    a = lax.dot_general(qb, kb, NT, preferred_element_type=F32) * mask_scr[0]
    tot = f
    pre = f
    post = jnp.ones_like(f)
    for lev in range(n_levels):
        b = 1 << lev
        if b >= SUBLANES:
            tot_lo, tot_up = _lower_halves(tot, b), _upper_halves(tot, b)
            pre_lo, pre_up = _lower_halves(pre, b), _upper_halves(pre, b)
            post_lo, post_up = _lower_halves(post, b), _upper_halves(post, b)
            zeros = jnp.zeros_like(tot_lo)
            qt = _interleave_halves(zeros, _upper_halves(q, b) * pre_up, b)
            kt = _interleave_halves(_lower_halves(k, b) * post_lo, zeros, b)
            pre = _interleave_halves(pre_lo, pre_up * tot_lo, b)
            post = _interleave_halves(post_lo * tot_up, post_up, b)
            both = tot_lo * tot_up
            tot = _interleave_halves(both, both, b)
        else:
            upper = (row & b) != 0
            grouped = tot.reshape(L // SUBLANES, SUBLANES, HEAD_W)
            sibling = pltpu.roll(grouped, b, 1).reshape(L, HEAD_W)
            if 2 * b != SUBLANES:
                sibling = jnp.where(upper, sibling, pltpu.roll(grouped, SUBLANES - b, 1).reshape(L, HEAD_W))
            qt = jnp.where(upper, q * pre, 0.0)
            kt = jnp.where(upper, 0.0, k * post)
            pre = pre * jnp.where(upper, sibling, 1.0)
            post = post * jnp.where(upper, 1.0, sibling)
            tot = tot * sibling
        p = lax.dot_general(qt.astype(BF16), kt.astype(BF16), NT, preferred_element_type=F32)
        a = a + (p if lev == n_levels - 1 else p * mask_scr[lev + 1])
        yield
    o = (jnp.dot(a.astype(BF16), v, preferred_element_type=F32)
         + lax.dot_general((q * pre).astype(BF16), st.astype(BF16), NT, preferred_element_type=F32))
    kd = (k * post).astype(BF16)
    result += [o, st * tot[0:1, :] + lax.dot_general(v, kd, TN, preferred_element_type=F32)]
    yield


def _hgrn_head(qb, kb, f, v, st, mask_scr):
    result = []
    for _ in _hgrn_head_stages(qb, kb, f, v, st, mask_scr, result):
        pass
    return result


def _b_scan_kernel(q_ref, k_ref, f_ref, v_ref, s0_ref, o_ref, sfin_ref, s_scr, mask_scr):
    chunk = pl.program_id(1)

    @pl.when(chunk == 0)
    def _():
        s_scr[...] = s0_ref[...]

    @pl.when((chunk == 0) & (pl.program_id(0) == 0))
    def _():
        _hgrn_init_masks(mask_scr)

    for hd in range(B_HEADS):
        o, s_scr[hd] = _hgrn_head(q_ref[hd], k_ref[hd], f_ref[hd], v_ref[hd], s_scr[hd], mask_scr)
        o_ref[hd] = o.astype(o_ref.dtype)

    @pl.when(chunk == pl.num_programs(1) - 1)
    def _():
        sfin_ref[...] = s_scr[...]


def _b_scan(q, k, f, v, s0, nb, seq):
    L = SCAN_CHUNK
    nc = seq // L
    heads = pl.BlockSpec((B_HEADS, L, HEAD_W), lambda b, i: (0, b * nc + i, 0))
    n_masks = int(math.log2(L))
    return pl.pallas_call(
        _b_scan_kernel,
        grid=(nb, nc),
        in_specs=[heads, heads, heads, heads, _const_spec(s0.shape)],
        out_specs=[heads, pl.BlockSpec(s0.shape, lambda b, i: (0, 0, 0))],
        out_shape=[jax.ShapeDtypeStruct((B_HEADS, nb * seq, HEAD_W), BF16), jax.ShapeDtypeStruct(s0.shape, F32)],
        scratch_shapes=[pltpu.VMEM(s0.shape, F32), pltpu.VMEM((n_masks, L, L), F32)],
        compiler_params=_params(2),
        name="hgrn_scan",
    )(q, k, f, v, s0)


def _interleave(main, filler=None):
    for n_filler_stages in main:
        for _ in range(n_filler_stages if filler is not None else 0):
            next(filler, None)
    if filler is not None:
        for _ in filler:
            pass


def _scan_ffn_kernel(*refs, kind, tiles_per_seq, n_tiles, sub):
    if kind == "a":
        (q_ref, k_ref, v_ref, og_ref, gates_ref, c0_ref, m0_ref, h_ref, wo_ref, gmix_ref,
         gpre_ref, gpost_ref, win_ref, wout_ref, out_ref, c_scr, m_scr, o_scr, u_ref, act_ref) = refs
    else:
        (q_ref, k_ref, f_ref, v_ref, s0_ref, h_ref, sg_ref, gn_ref, wo_ref, gmix_ref,
         gpre_ref, gpost_ref, win_ref, wout_ref, out_ref, s_scr, mask_scr, o_scr, u_ref, act_ref) = refs
    step = pl.program_id(0)
    L = SCAN_CHUNK
    chunks = _sub_tiles(h_ref.shape[0], L)

    def scan_stream(tile):
        first = tile % tiles_per_seq == 0
        dst = o_scr.at[tile % 2]
        for c, rows in enumerate(chunks):
            if kind == "a":
                yield from _mlstm_chunk_stages(q_ref, k_ref, v_ref, og_ref, gates_ref, rows, c_scr, m_scr,
                                               c0_ref, m0_ref, first if c == 0 else None, dst)
                continue
            for hd in range(B_HEADS):
                st = s_scr[hd]
                if c == 0:
                    st = jnp.where(first, s0_ref[hd], st)
                result = []
                yield from _hgrn_head_stages(q_ref[hd, rows, :], k_ref[hd, rows, :], f_ref[hd, rows, :],
                                             v_ref[hd, rows, :], st, mask_scr, result)
                dst[hd, rows, :] = result[0].astype(BF16)
                s_scr[hd] = result[1]

    def ffn_stream(tile, quota_in, quota_down):
        src = o_scr.at[tile % 2]
        mix_refs = (src, wo_ref, gmix_ref) if kind == "a" else (src, sg_ref, gn_ref, wo_ref, gmix_ref)
        prologue, _, _, epilogue = _ffn_stages(
            kind, h_ref, mix_refs, gpre_ref, gpost_ref, win_ref, wout_ref, out_ref, u_ref, act_ref)
        d_ff, d = wout_ref.shape
        tiles = _sub_tiles(h_ref.shape[0], sub)
        for rows in tiles:
            prologue(rows)
            yield 0
        for lo in range(0, d_ff, FF_TILE):
            for rows in tiles:
                gate = jnp.dot(u_ref[rows, :], win_ref[:, lo:lo + FF_TILE], preferred_element_type=F32)
                yield quota_in
                up = jnp.dot(u_ref[rows, :], win_ref[:, d_ff + lo:d_ff + lo + FF_TILE], preferred_element_type=F32)
                act_ref[rows, lo:lo + FF_TILE] = (gate * _sigmoid(gate) * up).astype(BF16)
                yield quota_in
        for rows in tiles:
            y = []
            for c in range(0, d, PROJ_TILE):
                y.append(jnp.dot(act_ref[rows, :], wout_ref[:, c:c + PROJ_TILE], preferred_element_type=F32))
                yield quota_down
            epilogue(rows, jnp.concatenate(y, axis=1))
            yield 0

    quota_in, quota_down = SCAN_STAGES_PER_FFN_DOT[kind]

    @pl.when(step == 0)
    def _():
        if kind == "a":
            c_scr[...] = c0_ref[...]
            m_scr[...] = m0_ref[...]
        else:
            _hgrn_init_masks(mask_scr)
            s_scr[...] = s0_ref[...]
        _interleave(scan_stream(0))

    @pl.when((step > 0) & (step < n_tiles))
    def _():
        _interleave(ffn_stream(step - 1, quota_in, quota_down), scan_stream(step))

    @pl.when(step == n_tiles)
    def _():
        _interleave(ffn_stream(n_tiles - 1, 0, 0))


def _scan_ffn(kind, scan_args, h, mix_args, g_pre, g_post, w_in, w_out, seq):
    m, d = h.shape
    d_ff = w_out.shape[0]
    tile = FUSED_ROWS
    n_tiles = m // tile
    scan_tile = lambda i: jnp.minimum(i, n_tiles - 1)
    ffn_row = lambda c: pl.BlockSpec((tile, c), lambda i: (jnp.maximum(i - 1, 0), 0))
    vec = _const_spec((1, d))
    if kind == "a":
        q, k, v, og, gates, c0, m0 = scan_args
        w_o, g_mix = mix_args
        srow = lambda c: pl.BlockSpec((tile, c), lambda i: (scan_tile(i), 0))
        scan_specs = [srow(q.shape[1]), srow(k.shape[1]), srow(v.shape[1]), srow(og.shape[1]),
                      srow(gates.shape[1]), _const_spec(c0.shape), _const_spec(m0.shape)]
        mix_specs = [_const_spec(w_o.shape), vec]
        scan_scratch = [pltpu.VMEM(c0.shape, F32), pltpu.VMEM(m0.shape, F32),
                        pltpu.VMEM((2, tile, v.shape[1]), BF16)]
        name = "mlstm_scan_ffn"
    else:
        q, k, f, v, s0 = scan_args
        sg, g_norm, w_o, g_mix = mix_args
        heads = pl.BlockSpec((B_HEADS, tile, HEAD_W), lambda i: (0, scan_tile(i), 0))
        scan_specs = [heads, heads, heads, heads, _const_spec(s0.shape)]
        mix_specs = [ffn_row(sg.shape[1]), _const_spec(g_norm.shape), _const_spec(w_o.shape), vec]
        scan_scratch = [pltpu.VMEM(s0.shape, F32),
                        pltpu.VMEM((int(math.log2(SCAN_CHUNK)), SCAN_CHUNK, SCAN_CHUNK), F32),
                        pltpu.VMEM((2, B_HEADS, tile, HEAD_W), BF16)]
        name = "hgrn_scan_ffn"
    return pl.pallas_call(
        functools.partial(_scan_ffn_kernel, kind=kind, tiles_per_seq=seq // tile, n_tiles=n_tiles, sub=FFN_SUB_ROWS),
        grid=(n_tiles + 1,),
        in_specs=scan_specs + [ffn_row(d)] + mix_specs + [vec, vec, w_in.spec(), w_out.spec()],
        out_specs=ffn_row(d),
        out_shape=jax.ShapeDtypeStruct((m, d), F32),
        scratch_shapes=scan_scratch + [pltpu.VMEM((tile, d), BF16), pltpu.VMEM((tile, d_ff), BF16)],
        compiler_params=_params(1),
        name=name,
    )(*scan_args, h, *mix_args, g_pre, g_post, w_in.array, w_out.array)


def _pad_rows(x, axis, n, value=0.0):
    pad = [(0, 0)] * x.ndim
    pad[axis] = (0, n - x.shape[axis])
    return jnp.pad(x, pad, constant_values=value)


def _trunk(h, p, state, nb, seq):
    gn = p["gains"]
    scan_seq = max(seq, SCAN_CHUNK)
    short = scan_seq != seq

    h = _ffn(h, gn[0][0], gn[0][1], p["ffn_in"][0][0], p["ffn_out"][0][0])
    q, k, v, og, gates, tail = _a_proj(h, gn[0][2], p["a_w"], p["a_wg"], p["a_cw"], p["a_cb"], p["a_gb"],
                                      state["tail"], seq)
    if short:
        q, k, v, og = (_pad_rows(t, 0, scan_seq) for t in (q, k, v, og))
        lane = jnp.arange(2 * LANES)[None, :]
        pad_gate = jnp.where(lane < LANES, MASKED, 0.0).astype(F32)
        gates = jnp.concatenate([gates, jnp.broadcast_to(pad_gate, (scan_seq - seq, 2 * LANES))], axis=0)
        y, c_fin, m_fin = _a_scan(q, k, v, og, gates, state["c"], state["m"], nb, scan_seq)
        h = _ffn(h, gn[0][4], gn[0][5], p["ffn_in"][0][1], p["ffn_out"][0][1],
                 mixer=("a", y[:seq], p["a_wo"], gn[0][3]))
    else:
        c_fin = m_fin = None
        h = _scan_ffn("a", (q, k, v, og, gates, state["c"], state["m"]), h, (p["a_wo"], gn[0][3]),
                      gn[0][4], gn[0][5], p["ffn_in"][0][1], p["ffn_out"][0][1], seq)

    h = _ffn(h, gn[1][0], gn[1][1], p["ffn_in"][1][0], p["ffn_out"][1][0])
    q, k, f, v, sg = _b_proj(h, gn[1][2], p["b_w"], p["b_fb"], p["b_lbraw"], seq)
    if short:
        q, k, v = (_pad_rows(t, 1, scan_seq) for t in (q, k, v))
        f = _pad_rows(f, 1, scan_seq, 1.0)
        o, s_fin = _b_scan(q, k, f, v, state["s"], nb, scan_seq)
        h = _ffn(h, gn[1][4], gn[1][5], p["ffn_in"][1][1], p["ffn_out"][1][1],
                 mixer=("b", o[:, :seq], sg, p["b_gn"], p["b_wo"], gn[1][3]))
    else:
        s_fin = None
        h = _scan_ffn("b", (q, k, f, v, state["s"]), h, (sg, p["b_gn"], p["b_wo"], gn[1][3]), gn[1][4], gn[1][5],
                      p["ffn_in"][1][1], p["ffn_out"][1][1], seq)
    return h, {"tail": tail, "c": c_fin, "m": m_fin, "s": s_fin}


def kernel(x, meta_tokens, norm_gains, ffn_w_in, ffn_w_out, a_w_in, a_conv_w, a_conv_b, a_gate_b, a_w_out,
           b_w_in, b_f_bias, b_lb_raw, b_g_norm, b_w_out):
    bsz, seq, d = x.shape
    depth = norm_gains.shape[0]
    assert depth == 2 and a_w_in.shape[0] == 1 and b_w_in.shape[0] == 1
    assert ffn_w_out.shape[2] % FF_TILE == 0
    d_qkvo = 2 * A_HEADS * A_DK + 2 * A_HEADS * HEAD_W
    ffn_in_bf16 = ffn_w_in.astype(BF16)
    ffn_out_bf16 = ffn_w_out.astype(BF16)

    def gate_lanes(t):
        z = jnp.zeros(t.shape[:-1] + (LANES - A_HEADS,), t.dtype)
        return jnp.concatenate([t[..., :A_HEADS], z, t[..., A_HEADS:], z], axis=-1)

    p = {
        "gains": [[norm_gains[i, j][None, :] for j in range(6)] for i in range(depth)],
        "ffn_in": [[_Stacked(ffn_in_bf16, i, j) for j in range(2)] for i in range(depth)],
        "ffn_out": [[_Stacked(ffn_out_bf16, i, j) for j in range(2)] for i in range(depth)],
        "a_w": a_w_in[0][:, :d_qkvo].astype(BF16),
        "a_wg": gate_lanes(a_w_in[0][:, d_qkvo:]).astype(BF16),
        "a_cw": a_conv_w[0], "a_cb": a_conv_b[0][None, :],
        "a_gb": gate_lanes(a_gate_b[0][None, :]),
        "a_wo": a_w_out[0].astype(BF16),
        "b_w": b_w_in[0].astype(BF16), "b_fb": b_f_bias[0][None, :], "b_lbraw": b_lb_raw,
        "b_gn": b_g_norm[0][None, :], "b_wo": b_w_out[0].astype(BF16),
    }
    state0 = {
        "tail": jnp.zeros((SUBLANES, 2 * A_HEADS * A_DK), F32),
        "c": jnp.zeros((A_HEADS, LANES, 2 * HEAD_W), F32),
        "m": jnp.zeros((1, LANES), F32),
        "s": jnp.zeros((B_HEADS, HEAD_W, HEAD_W), F32),
    }
    _, state = _trunk(meta_tokens.astype(x.dtype), p, state0, 1, N_META)
    out, _ = _trunk(x.reshape(bsz * seq, d), p, state, bsz, seq)
    return out.reshape(bsz, seq, d)
```

```python
import functools
import math

import jax
import jax.numpy as jnp
from jax import lax
from jax.experimental import pallas as pl
from jax.experimental.pallas import tpu as pltpu

LANES = 128
SUBLANES = 8
VMEM_LIMIT_BYTES = 56 * 1024 * 1024

N_META = 16
EPS = 1e-6
A_HEADS = 8
A_DK = 64
A_CONV = 4
B_HEADS = 8
HEAD_W = 128
SCAN_CHUNK = 128
FF_TILE = 256
FFN_ROWS = 1024
FFN_SUB_ROWS = 256
FUSED_ROWS = 512
SCAN_STAGES_PER_FFN_DOT = {"a": (1, 5), "b": (2, 5)}
HEADS_IN_LOCKSTEP = 2
FFN_OVERLAP_AT = 1
PROJ_ROWS = 1024
PROJ_SUB_ROWS = 256
PROJ_TILE = 256
MASKED = -1e30

F32 = jnp.float32
BF16 = jnp.bfloat16
NT = (((1,), (1,)), ((), ()))
TN = (((0,), (0,)), ((), ()))

assert SCAN_CHUNK == LANES and 2 * A_DK == LANES and PROJ_TILE == 2 * HEAD_W


def _rmsnorm(x, g):
    return x * lax.rsqrt(jnp.mean(x * x, axis=-1, keepdims=True) + EPS) * g


def _sigmoid(x):
    return 1.0 / (1.0 + jnp.exp(-x))


def _log_sigmoid(x):
    return jnp.minimum(x, 0.0) - jnp.log1p(jnp.exp(-jnp.abs(x)))


def _params(n_grid_axes):
    return pltpu.CompilerParams(dimension_semantics=("arbitrary",) * n_grid_axes,
                                vmem_limit_bytes=VMEM_LIMIT_BYTES)


def _const_spec(shape):
    zeros = (0,) * len(shape)
    return pl.BlockSpec(shape, lambda *_: zeros, pipeline_mode=pl.Buffered(1))


class _Stacked:
    def __init__(self, array, *index):
        self.array, self.index = array, index
        self.shape = array.shape[len(index):]

    def spec(self):
        at = self.index + (0,) * len(self.shape)
        return pl.BlockSpec((None,) * len(self.index) + self.shape, lambda *_: at, pipeline_mode=pl.Buffered(1))


def _sub_tiles(n_rows, sub):
    return [slice(r0, r0 + sub) for r0 in range(0, n_rows, sub)]


def _ffn_stages(mixer, h_ref, mix_refs, gpre_ref, gpost_ref, win_ref, wout_ref, o_ref, u_ref, act_ref):
    d_ff = wout_ref.shape[0]

    def ffn_input(rows):
        return h_ref[rows, :] if mixer is None else o_ref[rows, :]

    def prologue(rows):
        if mixer == "a":
            y_ref, wo_ref, gmix_ref = mix_refs
            mix = y_ref[rows, :]
        elif mixer == "b":
            o8_ref, sg_ref, gn_ref, wo_ref, gmix_ref = mix_refs
            o = jnp.concatenate([o8_ref[hd, rows, :].astype(F32) for hd in range(B_HEADS)], axis=-1)
            mix = _rmsnorm(o * sg_ref[rows, :].astype(F32), gn_ref[...]).astype(BF16)
        if mixer is not None:
            z = jnp.dot(mix, wo_ref[...], preferred_element_type=F32)
            o_ref[rows, :] = h_ref[rows, :] + _rmsnorm(z, gmix_ref[...])
        u_ref[rows, :] = _rmsnorm(ffn_input(rows), gpre_ref[...]).astype(BF16)

    def activation(rows, lo):
        gate = jnp.dot(u_ref[rows, :], win_ref[:, lo:lo + FF_TILE], preferred_element_type=F32)
        up = jnp.dot(u_ref[rows, :], win_ref[:, d_ff + lo:d_ff + lo + FF_TILE], preferred_element_type=F32)
        act_ref[rows, lo:lo + FF_TILE] = (gate * _sigmoid(gate) * up).astype(BF16)

    def down(rows):
        return jnp.dot(act_ref[rows, :], wout_ref[...], preferred_element_type=F32)

    def epilogue(rows, y):
        o_ref[rows, :] = ffn_input(rows) + 0.5 * _rmsnorm(y, gpost_ref[...])

    return prologue, activation, down, epilogue


def _ffn_kernel(*refs, mixer, sub):
    n_mix = {None: 0, "a": 3, "b": 5}[mixer]
    h_ref = refs[0]
    mix_refs = refs[1:1 + n_mix]
    gpre_ref, gpost_ref, win_ref, wout_ref, o_ref, u_ref, act_ref, y_scr = refs[1 + n_mix:]
    prologue, activation, down, epilogue = _ffn_stages(
        mixer, h_ref, mix_refs, gpre_ref, gpost_ref, win_ref, wout_ref, o_ref, u_ref, act_ref)
    tiles = _sub_tiles(h_ref.shape[0], sub)

    prologue(tiles[0])
    for t, rows in enumerate(tiles):
        for n, lo in enumerate(range(0, wout_ref.shape[0], FF_TILE)):
            activation(rows, lo)
            if n == FFN_OVERLAP_AT:
                if t + 1 < len(tiles):
                    prologue(tiles[t + 1])
                if t > 0:
                    epilogue(tiles[t - 1], y_scr[tiles[t - 1], :])
        y_scr[rows, :] = down(rows)
    epilogue(tiles[-1], y_scr[tiles[-1], :])


def _ffn(h, g_pre, g_post, w_in, w_out, mixer=None):
    m, d = h.shape
    d_ff = w_out.shape[0]
    tm = min(m, FFN_ROWS)
    row = lambda c: pl.BlockSpec((tm, c), lambda i: (i, 0))
    vec = _const_spec((1, d))
    if mixer is None:
        kind, name = None, "ffn"
        args, specs = [], []
    elif mixer[0] == "a":
        kind, name = "a", "mlstm_out_ffn"
        _, y, w_o, g_mix = mixer
        args = [y, w_o, g_mix]
        specs = [row(y.shape[1]), _const_spec(w_o.shape), vec]
    else:
        kind, name = "b", "hgrn_out_ffn"
        _, o8, sg, g_norm, w_o, g_mix = mixer
        args = [o8, sg, g_norm, w_o, g_mix]
        specs = [pl.BlockSpec((B_HEADS, tm, HEAD_W), lambda i: (0, i, 0)), row(sg.shape[1]),
                 _const_spec(g_norm.shape), _const_spec(w_o.shape), vec]
    return pl.pallas_call(
        functools.partial(_ffn_kernel, mixer=kind, sub=min(tm, FFN_SUB_ROWS)),
        grid=(m // tm,),
        in_specs=[row(d)] + specs + [vec, vec, w_in.spec(), w_out.spec()],
        out_specs=row(d),
        out_shape=jax.ShapeDtypeStruct((m, d), F32),
        scratch_shapes=[pltpu.VMEM((tm, d), BF16), pltpu.VMEM((tm, d_ff), BF16), pltpu.VMEM((tm, d), F32)],
        compiler_params=_params(1),
        name=name,
    )(h, *args, g_pre, g_post, w_in.array, w_out.array)


def _a_proj_kernel(h_ref, g_ref, w_ref, wg_ref, cw_ref, cb_ref, gb_ref, tail0_ref,
                   q_ref, k_ref, v_ref, og_ref, gates_ref, tail_ref, z_scr, u_ref, *, tiles_per_seq, sub):
    d_qk = A_HEADS * A_DK
    d_v = A_HEADS * HEAD_W

    @pl.when(pl.program_id(0) % tiles_per_seq == 0)
    def _():
        z_scr[:SUBLANES, :] = tail0_ref[...]

    for rows in _sub_tiles(h_ref.shape[0], sub):
        u_ref[rows, :] = _rmsnorm(h_ref[rows, :], g_ref[...]).astype(BF16)
        for c in range(0, 2 * d_qk, PROJ_TILE):
            cols = slice(c, c + PROJ_TILE)
            z = jnp.dot(u_ref[rows, :], w_ref[:, cols], preferred_element_type=F32)
            z_scr[SUBLANES:, cols] = z
            acc = cb_ref[:, cols] + cw_ref[A_CONV - 1:A_CONV, cols] * z
            for j in range(1, A_CONV):
                acc = acc + cw_ref[A_CONV - 1 - j:A_CONV - j, cols] * z_scr[SUBLANES - j:SUBLANES - j + sub, cols]
            z_scr[:SUBLANES, cols] = z[sub - SUBLANES:, :]
            act = acc * _sigmoid(acc)
            if c < d_qk:
                q_ref[rows, cols] = (act * (A_DK ** -0.5)).astype(BF16)
            else:
                k_ref[rows, c - d_qk:c - d_qk + PROJ_TILE] = act.astype(BF16)
            v_ref[rows, cols] = jnp.dot(u_ref[rows, :], w_ref[:, 2 * d_qk + c:2 * d_qk + c + PROJ_TILE],
                                        preferred_element_type=F32).astype(BF16)
            og = jnp.dot(u_ref[rows, :], w_ref[:, 2 * d_qk + d_v + c:2 * d_qk + d_v + c + PROJ_TILE],
                         preferred_element_type=F32)
            og_ref[rows, cols] = _sigmoid(og).astype(BF16)
        g = jnp.dot(u_ref[rows, :], wg_ref[...], preferred_element_type=F32) + gb_ref[...]
        gates_ref[rows, :LANES] = g[:, :LANES]
        gates_ref[rows, LANES:] = _log_sigmoid(g[:, LANES:])
    tail_ref[...] = z_scr[:SUBLANES, :]


def _a_proj(h, g, w, wg, cw, cb, gb, tail0, seq):
    m, d = h.shape
    tm = min(seq, PROJ_ROWS)
    sub = min(tm, PROJ_SUB_ROWS)
    d_qk = A_HEADS * A_DK
    d_v = A_HEADS * HEAD_W
    assert d_v == 2 * d_qk
    row = lambda c: pl.BlockSpec((tm, c), lambda i: (i, 0))
    return pl.pallas_call(
        functools.partial(_a_proj_kernel, tiles_per_seq=seq // tm, sub=sub),
        grid=(m // tm,),
        in_specs=[row(d), _const_spec((1, d)), _const_spec(w.shape), _const_spec(wg.shape),
                  _const_spec(cw.shape), _const_spec(cb.shape), _const_spec(gb.shape), _const_spec(tail0.shape)],
        out_specs=[row(d_qk), row(d_qk), row(d_v), row(d_v), row(2 * LANES),
                   pl.BlockSpec((SUBLANES, 2 * d_qk), lambda i: (0, 0))],
        out_shape=[jax.ShapeDtypeStruct((m, d_qk), BF16), jax.ShapeDtypeStruct((m, d_qk), BF16),
                   jax.ShapeDtypeStruct((m, d_v), BF16), jax.ShapeDtypeStruct((m, d_v), BF16),
                   jax.ShapeDtypeStruct((m, 2 * LANES), F32), jax.ShapeDtypeStruct((SUBLANES, 2 * d_qk), F32)],
        scratch_shapes=[pltpu.VMEM((SUBLANES + sub, 2 * d_qk), F32), pltpu.VMEM((tm, d), BF16)],
        compiler_params=_params(1),
        name="mlstm_proj",
    )(h, g, w, wg, cw, cb, gb, tail0)


def _mlstm_chunk_stages(q_ref, k_ref, v_ref, og_ref, gates_ref, rows, c_scr, m_scr, c0_ref, m0_ref, reset, y_ref):
    L = SCAN_CHUNK
    r = lax.broadcasted_iota(jnp.int32, (L, L), 0)
    s = lax.broadcasted_iota(jnp.int32, (L, L), 1)
    causal = s <= r
    tri = jnp.where(causal, 1.0, 0.0).astype(F32)
    fcum = jnp.dot(tri, gates_ref[rows, LANES:], preferred_element_type=F32, precision=lax.Precision.HIGHEST)
    a = gates_ref[rows, :LANES] - fcum
    row = lax.broadcasted_iota(jnp.int32, (L, LANES), 0)
    run = a
    for lev in range(int(math.log2(L))):
        b = 1 << lev
        run = jnp.maximum(run, jnp.where(row >= b, pltpu.roll(run, b, 0), MASKED))
    m_prev = m_scr[...]
    if reset is not None:
        m_prev = jnp.where(reset, m0_ref[...], m_prev)
    rmax = jnp.maximum(run, m_prev)
    mstab = fcum + rmax
    r_last = rmax[L - 1:L, :]
    decay = jnp.exp(m_prev - r_last)
    a_t = a.T
    ws_t = jnp.exp(a - r_last).T
    m_scr[...] = mstab[L - 1:L, :]
    yield

    key_half = lax.broadcasted_iota(jnp.int32, (LANES, 1), 0) // A_DK
    lane_half = lax.broadcasted_iota(jnp.int32, (1, LANES), 1) // A_DK
    ones = jnp.ones((L, HEAD_W), BF16)
    eye = jnp.where(r == s, 1.0, 0.0).astype(BF16)

    for pair in range(A_HEADS // 2):
        qp = q_ref[rows, pair * LANES:(pair + 1) * LANES]
        kp = k_ref[rows, pair * LANES:(pair + 1) * LANES]
        zero = jnp.zeros_like(kp)
        k_blocks = jnp.concatenate([jnp.where(lane_half == 0, kp, zero),
                                    jnp.where(lane_half == 1, kp, zero)], axis=0)
        scores = lax.dot_general(qp, k_blocks, NT, preferred_element_type=F32)
        k_t = lax.dot_general(eye, kp, NT, preferred_element_type=F32)
        yield
        for j in range(2):
            h = 2 * pair + j
            cols = slice(h * HEAD_W, (h + 1) * HEAD_W)
            rb = jnp.broadcast_to(rmax[:, h:h + 1], (L, L))
            w_intra = jnp.exp(jnp.where(causal, a_t[h:h + 1, :] - rb, MASKED))
            w_inter = jnp.exp(m_prev[:, h:h + 1] - rb)
            sc = scores[:, j * L:(j + 1) * L] * w_intra
            q_in = jnp.where(lane_half == j, qp.astype(F32) * w_inter, 0.0)
            v_aug = jnp.concatenate([v_ref[rows, cols], ones], axis=1)
            c_aug = c_scr[h]
            if reset is not None:
                c_aug = jnp.where(reset, c0_ref[h], c_aug)
            lhs = jnp.concatenate([sc.astype(BF16), q_in.astype(BF16)], axis=1)
            rhs = jnp.concatenate([v_aug, c_aug.astype(BF16)], axis=0)
            nd = jnp.dot(lhs, rhs, preferred_element_type=F32)
            yield
            mb = jnp.broadcast_to(mstab[:, h:h + 1], (L, HEAD_W))
            hb = nd[:, :HEAD_W] / jnp.maximum(jnp.abs(nd[:, HEAD_W:]), jnp.exp(-mb))
            y_ref[rows, cols] = (hb * og_ref[rows, cols].astype(F32)).astype(y_ref.dtype)
            kw_t = jnp.where(key_half == j, k_t, 0.0) * ws_t[h:h + 1, :]
            c_scr[h] = decay[:, h:h + 1] * c_aug + jnp.dot(kw_t.astype(BF16), v_aug, preferred_element_type=F32)
            yield


def _a_scan_kernel(q_ref, k_ref, v_ref, og_ref, gates_ref, c0_ref, m0_ref,
                   y_ref, cfin_ref, mfin_ref, c_scr, m_scr):
    chunk = pl.program_id(1)

    @pl.when(chunk == 0)
    def _():
        c_scr[...] = c0_ref[...]
        m_scr[...] = m0_ref[...]

    for _ in _mlstm_chunk_stages(q_ref, k_ref, v_ref, og_ref, gates_ref, slice(0, SCAN_CHUNK),
                                 c_scr, m_scr, c0_ref, m0_ref, None, y_ref):
        pass

    @pl.when(chunk == pl.num_programs(1) - 1)
    def _():
        cfin_ref[...] = c_scr[...]
        mfin_ref[...] = m_scr[...]


def _a_scan(q, k, v, og, gates, c0, m0, nb, seq):
    L = SCAN_CHUNK
    nc = seq // L
    row = lambda c: pl.BlockSpec((L, c), lambda b, i: (b * nc + i, 0))
    d_v = v.shape[1]
    return pl.pallas_call(
        _a_scan_kernel,
        grid=(nb, nc),
        in_specs=[row(q.shape[1]), row(k.shape[1]), row(d_v), row(d_v), row(2 * LANES),
                  _const_spec(c0.shape), _const_spec(m0.shape)],
        out_specs=[row(d_v), pl.BlockSpec(c0.shape, lambda b, i: (0, 0, 0)),
                   pl.BlockSpec(m0.shape, lambda b, i: (0, 0))],
        out_shape=[jax.ShapeDtypeStruct((nb * seq, d_v), BF16), jax.ShapeDtypeStruct(c0.shape, F32),
                   jax.ShapeDtypeStruct(m0.shape, F32)],
        scratch_shapes=[pltpu.VMEM(c0.shape, F32), pltpu.VMEM(m0.shape, F32)],
        compiler_params=_params(2),
        name="mlstm_scan",
    )(q, k, v, og, gates, c0, m0)


def _b_proj_kernel(h_ref, g_ref, w_ref, fb_ref, lbraw_ref, q_ref, k_ref, f_ref, v_ref, sg_ref, u_ref, *, sub):
    d_f = B_HEADS * HEAD_W
    raw = lbraw_ref[...]
    e = jnp.exp(raw - jnp.max(raw, axis=0, keepdims=True))
    sm = e / jnp.sum(e, axis=0, keepdims=True)
    lb_all = (sm[0:1, :] + sm[1:2, :]) - sm[0:1, :]

    def put_heads(ref, rows, c, x):
        ref[c // HEAD_W, rows, :] = x[:, :HEAD_W]
        ref[c // HEAD_W + 1, rows, :] = x[:, HEAD_W:]

    for rows in _sub_tiles(h_ref.shape[0], sub):
        u_ref[rows, :] = _rmsnorm(h_ref[rows, :], g_ref[...]).astype(BF16)
        for c in range(0, d_f, PROJ_TILE):
            cols = slice(c, c + PROJ_TILE)
            lb = lb_all[:, cols]
            fpre = jnp.dot(u_ref[rows, :], w_ref[:, d_f + c:d_f + c + PROJ_TILE],
                           preferred_element_type=F32) + fb_ref[:, cols]
            en = jnp.exp(-jnp.abs(fpre))
            big = 1.0 / (1.0 + en)
            small = en * big
            pos = fpre >= 0.0
            put_heads(f_ref, rows, c, lb + (1.0 - lb) * jnp.where(pos, big, small))
            put_heads(k_ref, rows, c, ((1.0 - lb) * jnp.where(pos, small, big)).astype(BF16))
            v = jnp.dot(u_ref[rows, :], w_ref[:, 2 * d_f + c:2 * d_f + c + PROJ_TILE], preferred_element_type=F32)
            put_heads(v_ref, rows, c, v.astype(BF16))
            zq = jnp.dot(u_ref[rows, :], w_ref[:, cols], preferred_element_type=F32)
            put_heads(q_ref, rows, c, (zq * _sigmoid(zq)).astype(BF16))
            g = jnp.dot(u_ref[rows, :], w_ref[:, 3 * d_f + c:3 * d_f + c + PROJ_TILE], preferred_element_type=F32)
            sg_ref[rows, cols] = _sigmoid(g).astype(BF16)


def _b_proj(h, g, w, fb, lbraw, seq):
    m, d = h.shape
    tm = min(seq, PROJ_ROWS)
    heads = pl.BlockSpec((B_HEADS, tm, HEAD_W), lambda i: (0, i, 0))
    hshape = lambda dt: jax.ShapeDtypeStruct((B_HEADS, m, HEAD_W), dt)
    return pl.pallas_call(
        functools.partial(_b_proj_kernel, sub=min(tm, PROJ_SUB_ROWS)),
        grid=(m // tm,),
        in_specs=[pl.BlockSpec((tm, d), lambda i: (i, 0)), _const_spec((1, d)), _const_spec(w.shape),
                  _const_spec(fb.shape), _const_spec(lbraw.shape)],
        out_specs=[heads, heads, heads, heads, pl.BlockSpec((tm, B_HEADS * HEAD_W), lambda i: (i, 0))],
        out_shape=[hshape(BF16), hshape(BF16), hshape(F32), hshape(BF16),
                   jax.ShapeDtypeStruct((m, B_HEADS * HEAD_W), BF16)],
        scratch_shapes=[pltpu.VMEM((tm, d), BF16)],
        compiler_params=_params(1),
        name="hgrn_proj",
    )(h, g, w, fb, lbraw)


def _lower_halves(x, b):
    return jnp.concatenate([x[i:i + b] for i in range(0, x.shape[0], 2 * b)], axis=0)


def _upper_halves(x, b):
    return jnp.concatenate([x[i + b:i + 2 * b] for i in range(0, x.shape[0], 2 * b)], axis=0)


def _interleave_halves(lo, up, b):
    pieces = []
    for i in range(0, lo.shape[0], b):
        pieces += [lo[i:i + b], up[i:i + b]]
    return jnp.concatenate(pieces, axis=0)


def _hgrn_init_masks(mask_scr):
    L = SCAN_CHUNK
    r = lax.broadcasted_iota(jnp.int32, (L, L), 0)
    s = lax.broadcasted_iota(jnp.int32, (L, L), 1)
    mask_scr[0] = jnp.where(r == s, 1.0, 0.0)
    for lev in range(int(math.log2(L)) - 1):
        mask_scr[lev + 1] = jnp.where((r >> (lev + 1)) == (s >> (lev + 1)), 1.0, 0.0)


def _hgrn_head_stages(qb, kb, f, v, st, mask_scr, result):
    L = SCAN_CHUNK
    n_levels = int(math.log2(L))
    q = qb.astype(F32)
    k = kb.astype(F32)
    row = lax.broadcasted_iota(jnp.int32, (L, HEAD_W), 0)
    a = lax.dot_general(qb, kb, NT, preferred_element_type=F32) * mask_scr[0]
    tot = f
    pre = f
    post = jnp.ones_like(f)
    for lev in range(n_levels):
        b = 1 << lev
        if b >= SUBLANES:
            tot_lo, tot_up = _lower_halves(tot, b), _upper_halves(tot, b)
            pre_lo, pre_up = _lower_halves(pre, b), _upper_halves(pre, b)
            post_lo, post_up = _lower_halves(post, b), _upper_halves(post, b)
            zeros = jnp.zeros_like(tot_lo)
            qt = _interleave_halves(zeros, _upper_halves(q, b) * pre_up, b)
            kt = _interleave_halves(_lower_halves(k, b) * post_lo, zeros, b)
            pre = _interleave_halves(pre_lo, pre_up * tot_lo, b)
            post = _interleave_halves(post_lo * tot_up, post_up, b)
            both = tot_lo * tot_up
            tot = _interleave_halves(both, both, b)
        else:
            upper = (row & b) != 0
            grouped = tot.reshape(L // SUBLANES, SUBLANES, HEAD_W)
            sibling = pltpu.roll(grouped, b, 1).reshape(L, HEAD_W)
            if 2 * b != SUBLANES:
                sibling = jnp.where(upper, sibling, pltpu.roll(grouped, SUBLANES - b, 1).reshape(L, HEAD_W))
            qt = jnp.where(upper, q * pre, 0.0)
            kt = jnp.where(upper, 0.0, k * post)
            pre = pre * jnp.where(upper, sibling, 1.0)
            post = post * jnp.where(upper, 1.0, sibling)
            tot = tot * sibling
        p = lax.dot_general(qt.astype(BF16), kt.astype(BF16), NT, preferred_element_type=F32)
        a = a + (p if lev == n_levels - 1 else p * mask_scr[lev + 1])
        yield
    o = (jnp.dot(a.astype(BF16), v, preferred_element_type=F32)
         + lax.dot_general((q * pre).astype(BF16), st.astype(BF16), NT, preferred_element_type=F32))
    kd = (k * post).astype(BF16)
    result += [o, st * tot[0:1, :] + lax.dot_general(v, kd, TN, preferred_element_type=F32)]
    yield


def _hgrn_head(qb, kb, f, v, st, mask_scr):
    result = []
    for _ in _hgrn_head_stages(qb, kb, f, v, st, mask_scr, result):
        pass
    return result


def _b_scan_kernel(q_ref, k_ref, f_ref, v_ref, s0_ref, o_ref, sfin_ref, s_scr, mask_scr):
    chunk = pl.program_id(1)

    @pl.when(chunk == 0)
    def _():
        s_scr[...] = s0_ref[...]

    @pl.when((chunk == 0) & (pl.program_id(0) == 0))
    def _():
        _hgrn_init_masks(mask_scr)

    for hd in range(B_HEADS):
        o, s_scr[hd] = _hgrn_head(q_ref[hd], k_ref[hd], f_ref[hd], v_ref[hd], s_scr[hd], mask_scr)
        o_ref[hd] = o.astype(o_ref.dtype)

    @pl.when(chunk == pl.num_programs(1) - 1)
    def _():
        sfin_ref[...] = s_scr[...]


def _b_scan(q, k, f, v, s0, nb, seq):
    L = SCAN_CHUNK
    nc = seq // L
    heads = pl.BlockSpec((B_HEADS, L, HEAD_W), lambda b, i: (0, b * nc + i, 0))
    n_masks = int(math.log2(L))
    return pl.pallas_call(
        _b_scan_kernel,
        grid=(nb, nc),
        in_specs=[heads, heads, heads, heads, _const_spec(s0.shape)],
        out_specs=[heads, pl.BlockSpec(s0.shape, lambda b, i: (0, 0, 0))],
        out_shape=[jax.ShapeDtypeStruct((B_HEADS, nb * seq, HEAD_W), BF16), jax.ShapeDtypeStruct(s0.shape, F32)],
        scratch_shapes=[pltpu.VMEM(s0.shape, F32), pltpu.VMEM((n_masks, L, L), F32)],
        compiler_params=_params(2),
        name="hgrn_scan",
    )(q, k, f, v, s0)


def _interleave(main, filler=None):
    for n_filler_stages in main:
        for _ in range(n_filler_stages if filler is not None else 0):
            next(filler, None)
    if filler is not None:
        for _ in filler:
            pass


def _scan_ffn_kernel(*refs, kind, tiles_per_seq, n_tiles, sub):
    if kind == "a":
        (q_ref, k_ref, v_ref, og_ref, gates_ref, c0_ref, m0_ref, h_ref, wo_ref, gmix_ref,
         gpre_ref, gpost_ref, win_ref, wout_ref, out_ref, c_scr, m_scr, o_scr, u_ref, act_ref) = refs
    else:
        (q_ref, k_ref, f_ref, v_ref, s0_ref, h_ref, sg_ref, gn_ref, wo_ref, gmix_ref,
         gpre_ref, gpost_ref, win_ref, wout_ref, out_ref, s_scr, mask_scr, o_scr, u_ref, act_ref) = refs
    step = pl.program_id(0)
    L = SCAN_CHUNK
    chunks = _sub_tiles(h_ref.shape[0], L)

    def scan_stream(tile):
        first = tile % tiles_per_seq == 0
        dst = o_scr.at[tile % 2]
        for c, rows in enumerate(chunks):
            if kind == "a":
                yield from _mlstm_chunk_stages(q_ref, k_ref, v_ref, og_ref, gates_ref, rows, c_scr, m_scr,
                                               c0_ref, m0_ref, first if c == 0 else None, dst)
                continue
            for group in range(0, B_HEADS, HEADS_IN_LOCKSTEP):
                heads = range(group, group + HEADS_IN_LOCKSTEP)
                results, stages = [], []
                for hd in heads:
                    st = s_scr[hd]
                    if c == 0:
                        st = jnp.where(first, s0_ref[hd], st)
                    results.append([])
                    stages.append(_hgrn_head_stages(q_ref[hd, rows, :], k_ref[hd, rows, :], f_ref[hd, rows, :],
                                                    v_ref[hd, rows, :], st, mask_scr, results[-1]))
                for _ in zip(*stages):
                    yield
                for hd, (o, st_new) in zip(heads, results):
                    dst[hd, rows, :] = o.astype(BF16)
                    s_scr[hd] = st_new

    def ffn_stream(tile, quota_in, quota_down):
        src = o_scr.at[tile % 2]
        mix_refs = (src, wo_ref, gmix_ref) if kind == "a" else (src, sg_ref, gn_ref, wo_ref, gmix_ref)
        prologue, _, _, epilogue = _ffn_stages(
            kind, h_ref, mix_refs, gpre_ref, gpost_ref, win_ref, wout_ref, out_ref, u_ref, act_ref)
        d_ff, d = wout_ref.shape
        tiles = _sub_tiles(h_ref.shape[0], sub)
        for rows in tiles:
            prologue(rows)
            yield 0
        for lo in range(0, d_ff, FF_TILE):
            for rows in tiles:
                gate = jnp.dot(u_ref[rows, :], win_ref[:, lo:lo + FF_TILE], preferred_element_type=F32)
                yield quota_in
                up = jnp.dot(u_ref[rows, :], win_ref[:, d_ff + lo:d_ff + lo + FF_TILE], preferred_element_type=F32)
                act_ref[rows, lo:lo + FF_TILE] = (gate * _sigmoid(gate) * up).astype(BF16)
                yield quota_in
        for rows in tiles:
            y = []
            for c in range(0, d, PROJ_TILE):
                y.append(jnp.dot(act_ref[rows, :], wout_ref[:, c:c + PROJ_TILE], preferred_element_type=F32))
                yield quota_down
            epilogue(rows, jnp.concatenate(y, axis=1))
            yield 0

    quota_in, quota_down = SCAN_STAGES_PER_FFN_DOT[kind]

    @pl.when(step == 0)
    def _():
        if kind == "a":
            c_scr[...] = c0_ref[...]
            m_scr[...] = m0_ref[...]
        else:
            _hgrn_init_masks(mask_scr)
            s_scr[...] = s0_ref[...]
        _interleave(scan_stream(0))

    @pl.when((step > 0) & (step < n_tiles))
    def _():
        _interleave(ffn_stream(step - 1, quota_in, quota_down), scan_stream(step))

    @pl.when(step == n_tiles)
    def _():
        _interleave(ffn_stream(n_tiles - 1, 0, 0))


def _scan_ffn(kind, scan_args, h, mix_args, g_pre, g_post, w_in, w_out, seq):
    m, d = h.shape
    d_ff = w_out.shape[0]
    tile = FUSED_ROWS
    assert seq % tile == 0, "sequences must be whole fused tiles"
    n_tiles = m // tile
    scan_tile = lambda i: jnp.minimum(i, n_tiles - 1)
    ffn_row = lambda c: pl.BlockSpec((tile, c), lambda i: (jnp.maximum(i - 1, 0), 0))
    vec = _const_spec((1, d))
    if kind == "a":
        q, k, v, og, gates, c0, m0 = scan_args
        w_o, g_mix = mix_args
        srow = lambda c: pl.BlockSpec((tile, c), lambda i: (scan_tile(i), 0))
        scan_specs = [srow(q.shape[1]), srow(k.shape[1]), srow(v.shape[1]), srow(og.shape[1]),
                      srow(gates.shape[1]), _const_spec(c0.shape), _const_spec(m0.shape)]
        mix_specs = [_const_spec(w_o.shape), vec]
        scan_scratch = [pltpu.VMEM(c0.shape, F32), pltpu.VMEM(m0.shape, F32),
                        pltpu.VMEM((2, tile, v.shape[1]), BF16)]
        name = "mlstm_scan_ffn"
    else:
        q, k, f, v, s0 = scan_args
        sg, g_norm, w_o, g_mix = mix_args
        heads = pl.BlockSpec((B_HEADS, tile, HEAD_W), lambda i: (0, scan_tile(i), 0))
        scan_specs = [heads, heads, heads, heads, _const_spec(s0.shape)]
        mix_specs = [ffn_row(sg.shape[1]), _const_spec(g_norm.shape), _const_spec(w_o.shape), vec]
        scan_scratch = [pltpu.VMEM(s0.shape, F32),
                        pltpu.VMEM((int(math.log2(SCAN_CHUNK)), SCAN_CHUNK, SCAN_CHUNK), F32),
                        pltpu.VMEM((2, B_HEADS, tile, HEAD_W), BF16)]
        name = "hgrn_scan_ffn"
    return pl.pallas_call(
        functools.partial(_scan_ffn_kernel, kind=kind, tiles_per_seq=seq // tile, n_tiles=n_tiles, sub=FFN_SUB_ROWS),
        grid=(n_tiles + 1,),
        in_specs=scan_specs + [ffn_row(d)] + mix_specs + [vec, vec, w_in.spec(), w_out.spec()],
        out_specs=ffn_row(d),
        out_shape=jax.ShapeDtypeStruct((m, d), F32),
        scratch_shapes=scan_scratch + [pltpu.VMEM((tile, d), BF16), pltpu.VMEM((tile, d_ff), BF16)],
        compiler_params=_params(1),
        name=name,
    )(*scan_args, h, *mix_args, g_pre, g_post, w_in.array, w_out.array)


def _pad_rows(x, axis, n, value=0.0):
    pad = [(0, 0)] * x.ndim
    pad[axis] = (0, n - x.shape[axis])
    return jnp.pad(x, pad, constant_values=value)


def _trunk(h, p, state, nb, seq):
    gn = p["gains"]
    scan_seq = max(seq, SCAN_CHUNK)
    short = scan_seq != seq

    h = _ffn(h, gn[0][0], gn[0][1], p["ffn_in"][0][0], p["ffn_out"][0][0])
    q, k, v, og, gates, tail = _a_proj(h, gn[0][2], p["a_w"], p["a_wg"], p["a_cw"], p["a_cb"], p["a_gb"],
                                      state["tail"], seq)
    if short:
        q, k, v, og = (_pad_rows(t, 0, scan_seq) for t in (q, k, v, og))
        lane = jnp.arange(2 * LANES)[None, :]
        pad_gate = jnp.where(lane < LANES, MASKED, 0.0).astype(F32)
        gates = jnp.concatenate([gates, jnp.broadcast_to(pad_gate, (scan_seq - seq, 2 * LANES))], axis=0)
        y, c_fin, m_fin = _a_scan(q, k, v, og, gates, state["c"], state["m"], nb, scan_seq)
        h = _ffn(h, gn[0][4], gn[0][5], p["ffn_in"][0][1], p["ffn_out"][0][1],
                 mixer=("a", y[:seq], p["a_wo"], gn[0][3]))
    else:
        c_fin = m_fin = None
        h = _scan_ffn("a", (q, k, v, og, gates, state["c"], state["m"]), h, (p["a_wo"], gn[0][3]),
                      gn[0][4], gn[0][5], p["ffn_in"][0][1], p["ffn_out"][0][1], seq)

    h = _ffn(h, gn[1][0], gn[1][1], p["ffn_in"][1][0], p["ffn_out"][1][0])
    q, k, f, v, sg = _b_proj(h, gn[1][2], p["b_w"], p["b_fb"], p["b_lbraw"], seq)
    if short:
        q, k, v = (_pad_rows(t, 1, scan_seq) for t in (q, k, v))
        f = _pad_rows(f, 1, scan_seq, 1.0)
        o, s_fin = _b_scan(q, k, f, v, state["s"], nb, scan_seq)
        h = _ffn(h, gn[1][4], gn[1][5], p["ffn_in"][1][1], p["ffn_out"][1][1],
                 mixer=("b", o[:, :seq], sg, p["b_gn"], p["b_wo"], gn[1][3]))
    else:
        s_fin = None
        h = _scan_ffn("b", (q, k, f, v, state["s"]), h, (sg, p["b_gn"], p["b_wo"], gn[1][3]), gn[1][4], gn[1][5],
                      p["ffn_in"][1][1], p["ffn_out"][1][1], seq)
    return h, {"tail": tail, "c": c_fin, "m": m_fin, "s": s_fin}


def kernel(x, meta_tokens, norm_gains, ffn_w_in, ffn_w_out, a_w_in, a_conv_w, a_conv_b, a_gate_b, a_w_out,
           b_w_in, b_f_bias, b_lb_raw, b_g_norm, b_w_out):
    bsz, seq, d = x.shape
    depth = norm_gains.shape[0]
    assert depth == 2 and a_w_in.shape[0] == 1 and b_w_in.shape[0] == 1
    assert ffn_w_out.shape[2] % FF_TILE == 0
    d_qkvo = 2 * A_HEADS * A_DK + 2 * A_HEADS * HEAD_W
    ffn_in_bf16 = ffn_w_in.astype(BF16)
    ffn_out_bf16 = ffn_w_out.astype(BF16)

    def gate_lanes(t):
        z = jnp.zeros(t.shape[:-1] + (LANES - A_HEADS,), t.dtype)
        return jnp.concatenate([t[..., :A_HEADS], z, t[..., A_HEADS:], z], axis=-1)

    p = {
        "gains": [[norm_gains[i, j][None, :] for j in range(6)] for i in range(depth)],
        "ffn_in": [[_Stacked(ffn_in_bf16, i, j) for j in range(2)] for i in range(depth)],
        "ffn_out": [[_Stacked(ffn_out_bf16, i, j) for j in range(2)] for i in range(depth)],
        "a_w": a_w_in[0][:, :d_qkvo].astype(BF16),
        "a_wg": gate_lanes(a_w_in[0][:, d_qkvo:]).astype(BF16),
        "a_cw": a_conv_w[0], "a_cb": a_conv_b[0][None, :],
        "a_gb": gate_lanes(a_gate_b[0][None, :]),
        "a_wo": a_w_out[0].astype(BF16),
        "b_w": b_w_in[0].astype(BF16), "b_fb": b_f_bias[0][None, :], "b_lbraw": b_lb_raw,
        "b_gn": b_g_norm[0][None, :], "b_wo": b_w_out[0].astype(BF16),
    }
    state0 = {
        "tail": jnp.zeros((SUBLANES, 2 * A_HEADS * A_DK), F32),
        "c": jnp.zeros((A_HEADS, LANES, 2 * HEAD_W), F32),
        "m": jnp.zeros((1, LANES), F32),
        "s": jnp.zeros((B_HEADS, HEAD_W, HEAD_W), F32),
    }
    _, state = _trunk(meta_tokens.astype(x.dtype), p, state0, 1, N_META)
    out, _ = _trunk(x.reshape(bsz * seq, d), p, state, bsz, seq)
    return out.reshape(bsz, seq, d)
```
